```python
import math
import jax, jax.numpy as jnp
from jax import lax
import numpy as np

D_MODEL = 1024
BATCH = 8
SEQ = 8192
DEPTH = 2
DEC_BATCH = 32
DEC_SEQ = 64
PAST_LEN = 4096

CHUNK = 64
WIDTH_A = 512
WIDTH_B = 512
MIX_WIDTH = WIDTH_A + WIDTH_B
SSM_GROUP_CH = 16
SSM_GROUPS = WIDTH_A // SSM_GROUP_CH
SSM_STATE = 64
POOL_WINDOWS = (2, 4, 8, 16)
POOL_GROUPS = len(POOL_WINDOWS)
POOL_GROUP_CH = WIDTH_B // POOL_GROUPS
POOL_HIST = max(POOL_WINDOWS) - 1
D_FF = 4 * D_MODEL
RMS_EPS = 1e-5
DT_MIN = 1e-3
DT_MAX = 1e-1

kernel_name = "s5_pool_hybrid_stream_step"


def rmsnorm(x, g):
    xf = x.astype(jnp.float32)
    y = xf * lax.rsqrt(jnp.mean(xf * xf, axis=-1, keepdims=True) + RMS_EPS)
    return (y * g.astype(jnp.float32)).astype(x.dtype)


def _scan_op(e1, e2):
    a1, b1 = e1
    a2, b2 = e2
    return a1 * a2, a2 * b1 + b2


def s5_mixer(u, h0, lam_re, lam_im, log_dt, b_re, b_im, c_re, c_im, d, w_glu):
    bsz, L, _ = u.shape
    f32 = jnp.float32
    lam = lax.complex(lam_re.astype(f32), lam_im.astype(f32))
    dt = jnp.exp(log_dt.astype(f32))[:, None]
    lam_dt = lam * dt
    lam_bar = jnp.exp(lam_dt)
    b = lax.complex(b_re.astype(f32), b_im.astype(f32))
    b_bar = ((lam_bar - 1.0) / lam)[..., None] * b
    c = lax.complex(c_re.astype(f32), c_im.astype(f32))
    blk = min(L, CHUNK)
    nblk = L // blk
    ug = u.astype(f32).reshape(bsz, nblk, blk, SSM_GROUPS, SSM_GROUP_CH).transpose(1, 0, 2, 3, 4)
    lam_pow = jnp.exp(lam_dt[None] * jnp.arange(1, blk + 1, dtype=f32)[:, None, None])
    a_elems = jnp.broadcast_to(lam_bar, (bsz, blk, SSM_GROUPS, SSM_STATE))

    def step(h, u_blk):
        bu = jnp.einsum('btgc,gpc->btgp', u_blk.astype(jnp.complex64), b_bar)
        _, hs = lax.associative_scan(_scan_op, (a_elems, bu), axis=1)
        hs = hs + lam_pow[None] * h[:, None]
        y = jnp.einsum('gcp,btgp->btgc', c, hs).real
        return hs[:, -1], y

    h_last, ys = lax.scan(step, h0, ug)
    y = ys.transpose(1, 0, 2, 3, 4).reshape(bsz, L, WIDTH_A)
    y = y + d.astype(f32) * u.astype(f32)
    z = jax.nn.gelu(y)
    out = z * jax.nn.sigmoid(z @ w_glu.astype(f32))
    return out.astype(u.dtype), h_last


def pool_mixer(u, hist, start_pos, w_pool, scale):
    bsz, L, _ = u.shape
    full = jnp.concatenate([hist.astype(u.dtype), u], axis=1)
    ff = full.astype(jnp.float32)
    cs0 = jnp.concatenate([jnp.zeros((bsz, 1, WIDTH_B), jnp.float32), lax.cumsum(ff, axis=1)], axis=1)
    pos = jnp.arange(L) + start_pos
    groups = []
    for k, w in enumerate(POOL_WINDOWS):
        sl = slice(k * POOL_GROUP_CH, (k + 1) * POOL_GROUP_CH)
        wsum = cs0[:, POOL_HIST + 1:POOL_HIST + 1 + L, sl] - cs0[:, POOL_HIST + 1 - w:POOL_HIST + 1 - w + L, sl]
        cnt = jnp.minimum(pos + 1, w).astype(jnp.float32)[None, :, None]
        groups.append(wsum / cnt)
    pooled = jnp.stack(groups, axis=2)
    diff = pooled - u.astype(jnp.float32).reshape(bsz, L, POOL_GROUPS, POOL_GROUP_CH)
    y = jnp.einsum('blkc,kcd->blkd', diff, w_pool.astype(jnp.float32)).reshape(bsz, L, WIDTH_B)
    y = y * scale.astype(jnp.float32)
    new_hist = full[:, -POOL_HIST:]
    return y.astype(u.dtype), new_hist


def trunk_layer(x, h0, pool_hist, start_pos, g_mix, g_ffn, w_in, lam_re, lam_im, log_dt,
                b_re, b_im, c_re, c_im, d, w_glu, w_pool, pool_scale, w_out, w_ff1, w_ff2):
    hn = rmsnorm(x, g_mix)
    proj = hn @ w_in
    u_a = proj[..., :WIDTH_A]
    u_b = proj[..., WIDTH_A:]
    y_a, h_last = s5_mixer(u_a, h0, lam_re, lam_im, log_dt, b_re, b_im, c_re, c_im, d, w_glu)
    y_b, new_hist = pool_mixer(u_b, pool_hist, start_pos, w_pool, pool_scale)
    x = x + jnp.concatenate([y_a, y_b], axis=-1) @ w_out
    hf = rmsnorm(x, g_ffn)
    x = x + jnp.square(jax.nn.relu(hf @ w_ff1)) @ w_ff2
    return x, h_last, new_hist


def setup_inputs(seed: int = 0) -> dict:
    key = jax.random.key(seed)
    ks = jax.random.split(key, 24)
    f32 = jnp.float32
    nrm = lambda k, s, sc: jax.random.normal(k, s, f32) * sc
    n_idx = jnp.arange(SSM_STATE, dtype=f32)
    lam_re = -0.5 + 0.01 * jax.random.normal(ks[5], (DEPTH, SSM_GROUPS, SSM_STATE), f32)
    lam_im = math.pi * n_idx[None, None, :] + 0.01 * jax.random.normal(ks[6], (DEPTH, SSM_GROUPS, SSM_STATE), f32)
    log_dt = jax.random.uniform(ks[7], (DEPTH, SSM_GROUPS), f32, math.log(DT_MIN), math.log(DT_MAX))
    return {
        "x_prompt": nrm(ks[0], (BATCH, SEQ, D_MODEL), 1.0),
        "x_sample": nrm(ks[1], (DEC_BATCH, DEC_SEQ, D_MODEL), 1.0),
        "state_ssm_re": nrm(ks[2], (DEPTH, DEC_BATCH, SSM_GROUPS, SSM_STATE), 0.1),
        "state_ssm_im": nrm(ks[3], (DEPTH, DEC_BATCH, SSM_GROUPS, SSM_STATE), 0.1),
        "state_pool": nrm(ks[4], (DEPTH, DEC_BATCH, POOL_HIST, WIDTH_B), 1.0),
        "g_mix": 1.0 + nrm(ks[8], (DEPTH, D_MODEL), 0.02),
        "g_ffn": 1.0 + nrm(ks[9], (DEPTH, D_MODEL), 0.02),
        "w_in": nrm(ks[10], (DEPTH, D_MODEL, MIX_WIDTH), D_MODEL ** -0.5),
        "ssm_lambda_re": lam_re,
        "ssm_lambda_im": lam_im,
        "ssm_log_dt": log_dt,
        "ssm_b_re": nrm(ks[11], (DEPTH, SSM_GROUPS, SSM_STATE, SSM_GROUP_CH), (2 * SSM_GROUP_CH) ** -0.5),
        "ssm_b_im": nrm(ks[12], (DEPTH, SSM_GROUPS, SSM_STATE, SSM_GROUP_CH), (2 * SSM_GROUP_CH) ** -0.5),
        "ssm_c_re": nrm(ks[13], (DEPTH, SSM_GROUPS, SSM_GROUP_CH, SSM_STATE), (2 * SSM_STATE) ** -0.5),
        "ssm_c_im": nrm(ks[14], (DEPTH, SSM_GROUPS, SSM_GROUP_CH, SSM_STATE), (2 * SSM_STATE) ** -0.5),
        "ssm_d": nrm(ks[15], (DEPTH, WIDTH_A), 1.0),
        "w_glu": nrm(ks[16], (DEPTH, WIDTH_A, WIDTH_A), WIDTH_A ** -0.5),
        "w_pool": nrm(ks[17], (DEPTH, POOL_GROUPS, POOL_GROUP_CH, POOL_GROUP_CH), POOL_GROUP_CH ** -0.5),
        "pool_scale": 1.0 + nrm(ks[18], (DEPTH, WIDTH_B), 0.05),
        "w_out": nrm(ks[19], (DEPTH, MIX_WIDTH, D_MODEL), MIX_WIDTH ** -0.5),
        "w_ff1": nrm(ks[20], (DEPTH, D_MODEL, D_FF), D_MODEL ** -0.5),
        "w_ff2": nrm(ks[21], (DEPTH, D_FF, D_MODEL), D_FF ** -0.5),
        "g_final": 1.0 + nrm(ks[22], (D_MODEL,), 0.02),
    }


def reference(x_prompt, x_sample, state_ssm_re, state_ssm_im, state_pool, g_mix, g_ffn, w_in,
              ssm_lambda_re, ssm_lambda_im, ssm_log_dt, ssm_b_re, ssm_b_im, ssm_c_re, ssm_c_im,
              ssm_d, w_glu, w_pool, pool_scale, w_out, w_ff1, w_ff2, g_final):
    f32 = jnp.float32
    xp = x_prompt
    xs = x_sample
    p_re, p_im, p_pool, s_re, s_im, s_pool = [], [], [], [], [], []
    for l in range(DEPTH):
        lw = (g_mix[l], g_ffn[l], w_in[l], ssm_lambda_re[l], ssm_lambda_im[l], ssm_log_dt[l],
              ssm_b_re[l], ssm_b_im[l], ssm_c_re[l], ssm_c_im[l], ssm_d[l], w_glu[l], w_pool[l],
              pool_scale[l], w_out[l], w_ff1[l], w_ff2[l])
        h0_p = jnp.zeros((xp.shape[0], SSM_GROUPS, SSM_STATE), jnp.complex64)
        hist_p = jnp.zeros((xp.shape[0], POOL_HIST, WIDTH_B), xp.dtype)
        xp, hp_last, hist_p_new = trunk_layer(xp, h0_p, hist_p, 0, *lw)
        p_re.append(hp_last.real)
        p_im.append(hp_last.imag)
        p_pool.append(hist_p_new)
        h0_s = lax.complex(state_ssm_re[l].astype(f32), state_ssm_im[l].astype(f32))
        xs, hs_last, hist_s_new = trunk_layer(xs, h0_s, state_pool[l], PAST_LEN, *lw)
        s_re.append(hs_last.real)
        s_im.append(hs_last.imag)
        s_pool.append(hist_s_new)
    y_prompt = rmsnorm(xp, g_final)
    y_sample = rmsnorm(xs, g_final)
    new_p_ssm_re = jnp.stack(p_re, axis=0)
    new_p_ssm_im = jnp.stack(p_im, axis=0)
    new_p_pool = jnp.stack(p_pool, axis=0)
    new_s_ssm_re = jnp.stack(s_re, axis=0)
    new_s_ssm_im = jnp.stack(s_im, axis=0)
    new_s_pool = jnp.stack(s_pool, axis=0)
    return (y_prompt, y_sample, new_p_ssm_re, new_p_ssm_im, new_p_pool, new_s_ssm_re, new_s_ssm_im, new_s_pool)
```

```python
import functools
import math

import jax
import jax.numpy as jnp
from jax import lax
from jax.experimental import pallas as pl
from jax.experimental.pallas import tpu as pltpu

D_MODEL = 1024
DEPTH = 2
PAST_LEN = 4096
WIDTH_A = 512
WIDTH_B = 512
SSM_GROUP_CH = 16
SSM_GROUPS = WIDTH_A // SSM_GROUP_CH
SSM_STATE = 64
STATE_W = SSM_GROUPS * SSM_STATE
POOL_WINDOWS = (2, 4, 8, 16)
POOL_GROUP_CH = WIDTH_B // len(POOL_WINDOWS)
POOL_HIST = max(POOL_WINDOWS) - 1
D_FF = 4 * D_MODEL
RMS_EPS = 1e-5

TILE_ROWS = 512
FF_CHUNK = 1024
BU_TILE = 256
BU_K = 128
C_TILE = 256
C_K = C_TILE // SSM_GROUP_CH * SSM_STATE
SCAN_VREG_ELEMS = 4 * 8 * 128
VMEM_LIMIT_BYTES = 60 * 1024 * 1024

_BF16 = jnp.bfloat16
_F32 = jnp.float32


def _rmsnorm(x, g):
    ms = jnp.mean(x * x, axis=-1, keepdims=True)
    return x * lax.rsqrt(ms + RMS_EPS) * g


def _layer_kernel(nb, tt, start_pos, final_norm,
                  x_ref, h0re_ref, h0im_ref, hist0_ref,
                  gmix_ref, gffn_ref, gfin_ref, win_ref, wb_ref, are_ref, aim_ref,
                  wcre_ref, wcim_ref, d_ref, wglu_ref, wpool_ref, pscale_ref,
                  wout_ref, wff1_ref, wff2_ref,
                  y_ref, hre_out, him_out, hist_out,
                  hre, him, ext, bu):
    rows = nb * tt
    hist_rows = POOL_HIST * nb
    i = pl.program_id(0)

    @pl.when(i == 0)
    def _():
        hre[...] = h0re_ref[...]
        him[...] = h0im_ref[...]
        ext[0:hist_rows, :] = hist0_ref[...]

    x = x_ref[...]
    hn = _rmsnorm(x, gmix_ref[...]).astype(_BF16)
    proj = jnp.dot(hn, win_ref[...], preferred_element_type=_F32)
    ua = proj[:, :WIDTH_A]
    ub = proj[:, WIDTH_A:]

    ua_bf = ua.astype(_BF16)
    n_half = STATE_W // BU_TILE
    for n in range(2 * n_half):
        kb = (n % n_half) * BU_TILE // SSM_STATE * SSM_GROUP_CH // BU_K
        bu[:, n * BU_TILE:(n + 1) * BU_TILE] = jnp.dot(
            ua_bf[:, kb * BU_K:(kb + 1) * BU_K], wb_ref[n], preferred_element_type=_F32)

    lc = SCAN_VREG_ELEMS // nb
    for c in range(STATE_W // lc):
        re_sl = slice(c * lc, (c + 1) * lc)
        im_sl = slice(STATE_W + c * lc, STATE_W + (c + 1) * lc)
        a_re = jnp.broadcast_to(are_ref[:, re_sl], (nb, lc))
        a_im = jnp.broadcast_to(aim_ref[:, re_sl], (nb, lc))

        def step(t, carry, re_sl=re_sl, im_sl=im_sl, a_re=a_re, a_im=a_im):
            h_r, h_i = carry
            r0 = pl.multiple_of(t * nb, nb)
            n_r = a_re * h_r - a_im * h_i + bu[pl.ds(r0, nb), re_sl]
            n_i = a_re * h_i + a_im * h_r + bu[pl.ds(r0, nb), im_sl]
            bu[pl.ds(r0, nb), re_sl] = n_r
            bu[pl.ds(r0, nb), im_sl] = n_i
            return n_r, n_i

        h_r, h_i = lax.fori_loop(0, tt, step, (hre[:, re_sl], him[:, re_sl]), unroll=4)
        hre[:, re_sl] = h_r
        him[:, re_sl] = h_i
    hre_out[...] = hre[...]
    him_out[...] = him[...]

    y_parts = []
    for j in range(WIDTH_A // C_TILE):
        h_re = bu[:, j * C_K:(j + 1) * C_K].astype(_BF16)
        h_im = bu[:, STATE_W + j * C_K:STATE_W + (j + 1) * C_K].astype(_BF16)
        y_parts.append(jnp.dot(h_re, wcre_ref[j], preferred_element_type=_F32)
                       + jnp.dot(h_im, wcim_ref[j], preferred_element_type=_F32))
    y = jnp.concatenate(y_parts, axis=1) + d_ref[...] * ua
    z = jax.nn.gelu(y)
    out_a = z * jax.nn.sigmoid(jnp.dot(z.astype(_BF16), wglu_ref[...], preferred_element_type=_F32))

    ext[hist_rows:hist_rows + rows, :] = ub
    pos = start_pos + i * tt + lax.broadcasted_iota(jnp.int32, (rows, POOL_GROUP_CH), 0) // nb
    yb_parts = []
    for k, w in enumerate(POOL_WINDOWS):
        ls = slice(k * POOL_GROUP_CH, (k + 1) * POOL_GROUP_CH)
        wsum = ext[hist_rows:hist_rows + rows, ls]
        for j in range(1, w):
            wsum = wsum + ext[(POOL_HIST - j) * nb:(POOL_HIST - j) * nb + rows, ls]
        cnt = jnp.minimum(pos + 1, w).astype(_F32)
        diff = wsum / cnt - ub[:, ls]
        yb_parts.append(jnp.dot(diff.astype(_BF16), wpool_ref[k], preferred_element_type=_F32))
    yb = jnp.concatenate(yb_parts, axis=1) * pscale_ref[...]
    new_hist = ext[rows:rows + hist_rows, :]
    hist_out[...] = new_hist
    ext[0:hist_rows, :] = new_hist

    mixed = jnp.concatenate([out_a, yb], axis=1).astype(_BF16)
    x1 = x + jnp.dot(mixed, wout_ref[...], preferred_element_type=_F32)

    hf = _rmsnorm(x1, gffn_ref[...]).astype(_BF16)
    acc = x1
    for c in range(D_FF // FF_CHUNK):
        mid = jnp.dot(hf, wff1_ref[:, c * FF_CHUNK:(c + 1) * FF_CHUNK], preferred_element_type=_F32)
        mid = jnp.square(jnp.maximum(mid, 0.0)).astype(_BF16)
        acc = acc + jnp.dot(mid, wff2_ref[c * FF_CHUNK:(c + 1) * FF_CHUNK, :], preferred_element_type=_F32)
    if final_norm:
        acc = _rmsnorm(acc, gfin_ref[...])
    y_ref[...] = acc


def _layer_call(x_tm, h0_re, h0_im, hist0, weights, *, nb, start_pos, final_norm):
    n_rows = x_tm.shape[0]
    tt = TILE_ROWS // nb
    assert n_rows % TILE_ROWS == 0 and tt >= POOL_HIST
    hist_rows = POOL_HIST * nb
    whole = pl.BlockSpec(memory_space=pltpu.VMEM)
    row_tile = pl.BlockSpec((TILE_ROWS, D_MODEL), lambda i: (i, 0))
    out_shape = (
        jax.ShapeDtypeStruct((n_rows, D_MODEL), _F32),
        jax.ShapeDtypeStruct((nb, STATE_W), _F32),
        jax.ShapeDtypeStruct((nb, STATE_W), _F32),
        jax.ShapeDtypeStruct((hist_rows, WIDTH_B), _F32),
    )
    return pl.pallas_call(
        functools.partial(_layer_kernel, nb, tt, start_pos, final_norm),
        grid=(n_rows // TILE_ROWS,),
        in_specs=[row_tile] + [whole] * (3 + len(weights)),
        out_specs=(row_tile, whole, whole, whole),
        out_shape=out_shape,
        scratch_shapes=[
            pltpu.VMEM((nb, STATE_W), _F32),
            pltpu.VMEM((nb, STATE_W), _F32),
            pltpu.VMEM((hist_rows + TILE_ROWS, WIDTH_B), _F32),
            pltpu.VMEM((TILE_ROWS, 2 * STATE_W), _F32),
        ],
        compiler_params=pltpu.CompilerParams(
            dimension_semantics=("arbitrary",), vmem_limit_bytes=VMEM_LIMIT_BYTES),
    )(x_tm, h0_re, h0_im, hist0, *weights)


def _prep_layer(g_mix, g_ffn, g_final, w_in, lam_re, lam_im, log_dt, b_re, b_im, c_re, c_im,
                d, w_glu, w_pool, pool_scale, w_out, w_ff1, w_ff2):
    lam = lax.complex(lam_re, lam_im)
    lam_bar = jnp.exp(lam * jnp.exp(log_dt)[:, None])
    b_bar = ((lam_bar - 1.0) / lam)[..., None] * lax.complex(b_re, b_im)
    eye = jnp.eye(SSM_GROUPS, dtype=_F32)

    def in_dense(bp):
        return jnp.einsum('gpc,gh->gchp', bp, eye).reshape(WIDTH_A, STATE_W)

    def out_dense(cp):
        return jnp.einsum('gcp,gh->gphc', cp, eye).reshape(STATE_W, WIDTH_A)

    b_dense = jnp.concatenate([in_dense(b_bar.real), in_dense(b_bar.imag)], axis=1)
    n_half = STATE_W // BU_TILE
    wb = []
    for n in range(2 * n_half):
        kb = (n % n_half) * BU_TILE // SSM_STATE * SSM_GROUP_CH // BU_K
        wb.append(b_dense[kb * BU_K:(kb + 1) * BU_K, n * BU_TILE:(n + 1) * BU_TILE])
    wb = jnp.stack(wb).astype(_BF16)
    c_re_d = out_dense(c_re)
    c_im_d = out_dense(-c_im)
    wc_re = jnp.stack([c_re_d[j * C_K:(j + 1) * C_K, j * C_TILE:(j + 1) * C_TILE]
                       for j in range(WIDTH_A // C_TILE)]).astype(_BF16)
    wc_im = jnp.stack([c_im_d[j * C_K:(j + 1) * C_K, j * C_TILE:(j + 1) * C_TILE]
                       for j in range(WIDTH_A // C_TILE)]).astype(_BF16)
    row = lambda v: v.reshape(1, -1).astype(_F32)
    return (row(g_mix), row(g_ffn), row(g_final), w_in.astype(_BF16), wb,
            row(lam_bar.real), row(lam_bar.imag), wc_re, wc_im, row(d),
            w_glu.astype(_BF16), w_pool.astype(_BF16), row(pool_scale),
            w_out.astype(_BF16), w_ff1.astype(_BF16), w_ff2.astype(_BF16))


def _to_time_major(x):
    b, l, d = x.shape
    return jnp.transpose(x, (1, 0, 2)).reshape(l * b, d)


def _from_time_major(x_tm, b):
    return jnp.transpose(x_tm.reshape(-1, b, x_tm.shape[-1]), (1, 0, 2))


def kernel(x_prompt, x_sample, state_ssm_re, state_ssm_im, state_pool, g_mix, g_ffn, w_in,
           ssm_lambda_re, ssm_lambda_im, ssm_log_dt, ssm_b_re, ssm_b_im, ssm_c_re, ssm_c_im,
           ssm_d, w_glu, w_pool, pool_scale, w_out, w_ff1, w_ff2, g_final):
    bp = x_prompt.shape[0]
    bs = x_sample.shape[0]
    xp = _to_time_major(x_prompt)
    xs = _to_time_major(x_sample)
    outs = {k: [] for k in ("p_re", "p_im", "p_pool", "s_re", "s_im", "s_pool")}
    for l in range(DEPTH):
        weights = _prep_layer(g_mix[l], g_ffn[l], g_final, w_in[l], ssm_lambda_re[l], ssm_lambda_im[l],
                              ssm_log_dt[l], ssm_b_re[l], ssm_b_im[l], ssm_c_re[l], ssm_c_im[l],
                              ssm_d[l], w_glu[l], w_pool[l], pool_scale[l], w_out[l], w_ff1[l], w_ff2[l])
        final_norm = l == DEPTH - 1
        zero_state = jnp.zeros((bp, STATE_W), _F32)
        xp, h_re, h_im, hist = _layer_call(
            xp, zero_state, zero_state, jnp.zeros((POOL_HIST * bp, WIDTH_B), _F32), weights,
            nb=bp, start_pos=0, final_norm=final_norm)
        outs["p_re"].append(h_re.reshape(bp, SSM_GROUPS, SSM_STATE))
        outs["p_im"].append(h_im.reshape(bp, SSM_GROUPS, SSM_STATE))
        outs["p_pool"].append(_from_time_major(hist, bp))
        xs, h_re, h_im, hist = _layer_call(
            xs, state_ssm_re[l].reshape(bs, STATE_W).astype(_F32),
            state_ssm_im[l].reshape(bs, STATE_W).astype(_F32),
            _to_time_major(state_pool[l]).astype(_F32), weights,
            nb=bs, start_pos=PAST_LEN, final_norm=final_norm)
        outs["s_re"].append(h_re.reshape(bs, SSM_GROUPS, SSM_STATE))
        outs["s_im"].append(h_im.reshape(bs, SSM_GROUPS, SSM_STATE))
        outs["s_pool"].append(_from_time_major(hist, bs))
    return (_from_time_major(xp, bp), _from_time_major(xs, bs),
            jnp.stack(outs["p_re"]), jnp.stack(outs["p_im"]), jnp.stack(outs["p_pool"]),
            jnp.stack(outs["s_re"]), jnp.stack(outs["s_im"]), jnp.stack(outs["s_pool"]))
```

```python
import functools

import jax
import jax.numpy as jnp
from jax import lax
from jax.experimental import pallas as pl
from jax.experimental.pallas import tpu as pltpu

D_MODEL = 1024
DEPTH = 2
PAST_LEN = 4096
WIDTH_A = 512
WIDTH_B = 512
MIX_WIDTH = WIDTH_A + WIDTH_B
SSM_GROUP_CH = 16
SSM_GROUPS = WIDTH_A // SSM_GROUP_CH
SSM_STATE = 64
STATE_W = SSM_GROUPS * SSM_STATE
POOL_WINDOWS = (2, 4, 8, 16)
POOL_GROUP_CH = WIDTH_B // len(POOL_WINDOWS)
POOL_HIST = max(POOL_WINDOWS) - 1
D_FF = 4 * D_MODEL
RMS_EPS = 1e-5

TILE_ROWS = 512
FF_CHUNK = 1024
BU_TILE = 256
BU_GROUPS = BU_TILE // SSM_STATE
BU_K = 128
N_BU_HALF = STATE_W // BU_TILE
C_TILE = 256
C_GROUPS = C_TILE // SSM_GROUP_CH
C_K = C_GROUPS * SSM_STATE
SCAN_VREG_ELEMS = 4 * 8 * 128
VMEM_LIMIT_BYTES = 60 * 1024 * 1024

_BF16 = jnp.bfloat16
_F32 = jnp.float32


def _rmsnorm(x, g):
    ms = jnp.mean(x * x, axis=-1, keepdims=True)
    return x * lax.rsqrt(ms + RMS_EPS) * g


def _bu_k_block(n):
    return (n % N_BU_HALF) * BU_GROUPS * SSM_GROUP_CH // BU_K


def _layer_kernel(nb, tt, start_pos, final_norm,
                  x_ref, h0re_ref, h0im_ref, hist0_ref,
                  gmix_ref, gffn_ref, gfin_ref, win_ref, wb_ref, are_ref, aim_ref,
                  wcre_ref, wcim_ref, d_ref, wglu_ref, wpool_ref, pscale_ref,
                  wout_ref, wff1_ref, wff2_ref,
                  y_ref, hre_out, him_out, hist_out,
                  hre, him, ext, bu):
    rows = nb * tt
    hist_rows = POOL_HIST * nb
    i = pl.program_id(0)

    @pl.when(i == 0)
    def _():
        hre[...] = h0re_ref[...]
        him[...] = h0im_ref[...]
        ext[0:hist_rows, :] = hist0_ref[...]

    x = x_ref[...].reshape(rows, D_MODEL)
    hn = _rmsnorm(x, gmix_ref[...]).astype(_BF16)
    proj = jnp.dot(hn, win_ref[...], preferred_element_type=_F32)
    proj = pltpu.einshape("btd->tbd", proj.reshape(nb, tt, MIX_WIDTH)).reshape(rows, MIX_WIDTH)
    ua = proj[:, :WIDTH_A]
    ub = proj[:, WIDTH_A:]

    ua_bf = ua.astype(_BF16)
    for n in range(2 * N_BU_HALF):
        kb = _bu_k_block(n)
        bu[:, n * BU_TILE:(n + 1) * BU_TILE] = jnp.dot(
            ua_bf[:, kb * BU_K:(kb + 1) * BU_K], wb_ref[n], preferred_element_type=_F32)

    lc = SCAN_VREG_ELEMS // nb
    for c in range(STATE_W // lc):
        re_sl = slice(c * lc, (c + 1) * lc)
        im_sl = slice(STATE_W + c * lc, STATE_W + (c + 1) * lc)
        a_re = jnp.broadcast_to(are_ref[:, re_sl], (nb, lc))
        a_im = jnp.broadcast_to(aim_ref[:, re_sl], (nb, lc))

        def step(t, carry, re_sl=re_sl, im_sl=im_sl, a_re=a_re, a_im=a_im):
            h_r, h_i = carry
            r0 = pl.multiple_of(t * nb, nb)
            n_r = a_re * h_r - a_im * h_i + bu[pl.ds(r0, nb), re_sl]
            n_i = a_re * h_i + a_im * h_r + bu[pl.ds(r0, nb), im_sl]
            bu[pl.ds(r0, nb), re_sl] = n_r
            bu[pl.ds(r0, nb), im_sl] = n_i
            return n_r, n_i

        h_r, h_i = lax.fori_loop(0, tt, step, (hre[:, re_sl], him[:, re_sl]), unroll=4)
        hre[:, re_sl] = h_r
        him[:, re_sl] = h_i
    hre_out[...] = hre[...]
    him_out[...] = him[...]

    y_parts = []
    for j in range(WIDTH_A // C_TILE):
        h_re = bu[:, j * C_K:(j + 1) * C_K].astype(_BF16)
        h_im = bu[:, STATE_W + j * C_K:STATE_W + (j + 1) * C_K].astype(_BF16)
        y_parts.append(jnp.dot(h_re, wcre_ref[j], preferred_element_type=_F32)
                       + jnp.dot(h_im, wcim_ref[j], preferred_element_type=_F32))
    y = jnp.concatenate(y_parts, axis=1) + d_ref[...] * ua
    z = jax.nn.gelu(y)
    out_a = z * jax.nn.sigmoid(jnp.dot(z.astype(_BF16), wglu_ref[...], preferred_element_type=_F32))

    ext[hist_rows:hist_rows + rows, :] = ub
    pos = start_pos + i * tt + lax.broadcasted_iota(jnp.int32, (rows, POOL_GROUP_CH), 0) // nb
    yb_parts = []
    for k, w in enumerate(POOL_WINDOWS):
        ls = slice(k * POOL_GROUP_CH, (k + 1) * POOL_GROUP_CH)
        wsum = ext[hist_rows:hist_rows + rows, ls]
        for j in range(1, w):
            wsum = wsum + ext[(POOL_HIST - j) * nb:(POOL_HIST - j) * nb + rows, ls]
        cnt = jnp.minimum(pos + 1, w).astype(_F32)
        diff = wsum / cnt - ub[:, ls]
        yb_parts.append(jnp.dot(diff.astype(_BF16), wpool_ref[k], preferred_element_type=_F32))
    yb = jnp.concatenate(yb_parts, axis=1) * pscale_ref[...]
    new_hist = ext[rows:rows + hist_rows, :]
    hist_out[...] = new_hist
    ext[0:hist_rows, :] = new_hist

    mixed = jnp.concatenate([out_a, yb], axis=1)
    mixed = pltpu.einshape("tbd->btd", mixed.reshape(tt, nb, MIX_WIDTH)).reshape(rows, MIX_WIDTH)
    x1 = x + jnp.dot(mixed.astype(_BF16), wout_ref[...], preferred_element_type=_F32)

    hf = _rmsnorm(x1, gffn_ref[...]).astype(_BF16)
    acc = x1
    for c in range(D_FF // FF_CHUNK):
        mid = jnp.dot(hf, wff1_ref[:, c * FF_CHUNK:(c + 1) * FF_CHUNK], preferred_element_type=_F32)
        mid = jnp.square(jnp.maximum(mid, 0.0)).astype(_BF16)
        acc = acc + jnp.dot(mid, wff2_ref[c * FF_CHUNK:(c + 1) * FF_CHUNK, :], preferred_element_type=_F32)
    if final_norm:
        acc = _rmsnorm(acc, gfin_ref[...])
    y_ref[...] = acc.reshape(nb, tt, D_MODEL)


def _layer_call(x, h0_re, h0_im, hist0, weights, layer, *, start_pos, final_norm):
    nb, seq, _ = x.shape
    tt = TILE_ROWS // nb
    assert seq % tt == 0 and tt >= POOL_HIST and tt % 8 == 0
    hist_rows = POOL_HIST * nb

    def resident(w):
        zeros = (0,) * (w.ndim - 1)
        return pl.BlockSpec((None,) + w.shape[1:], lambda i: (layer,) + zeros,
                            pipeline_mode=pl.Buffered(1))

    whole = pl.BlockSpec(memory_space=pltpu.VMEM)
    x_tile = pl.BlockSpec((nb, tt, D_MODEL), lambda i: (0, i, 0))
    out_shape = (
        jax.ShapeDtypeStruct(x.shape, _F32),
        jax.ShapeDtypeStruct((nb, STATE_W), _F32),
        jax.ShapeDtypeStruct((nb, STATE_W), _F32),
        jax.ShapeDtypeStruct((hist_rows, WIDTH_B), _F32),
    )
    return pl.pallas_call(
        functools.partial(_layer_kernel, nb, tt, start_pos, final_norm),
        grid=(seq // tt,),
        in_specs=[x_tile, whole, whole, whole] + [resident(w) for w in weights],
        out_specs=(x_tile, whole, whole, whole),
        out_shape=out_shape,
        scratch_shapes=[
            pltpu.VMEM((nb, STATE_W), _F32),
            pltpu.VMEM((nb, STATE_W), _F32),
            pltpu.VMEM((hist_rows + TILE_ROWS, WIDTH_B), _F32),
            pltpu.VMEM((TILE_ROWS, 2 * STATE_W), _F32),
        ],
        compiler_params=pltpu.CompilerParams(
            dimension_semantics=("arbitrary",), vmem_limit_bytes=VMEM_LIMIT_BYTES),
    )(x, h0_re, h0_im, hist0, *weights)


def _prep_weights(g_mix, g_ffn, g_final, w_in, lam_re, lam_im, log_dt, b_re, b_im, c_re, c_im,
                  d, w_glu, w_pool, pool_scale, w_out, w_ff1, w_ff2):
    depth = w_in.shape[0]
    dt = jnp.exp(log_dt)[..., None]
    mag = jnp.exp(lam_re * dt)
    a_re = mag * jnp.cos(lam_im * dt)
    a_im = mag * jnp.sin(lam_im * dt)
    den = lam_re * lam_re + lam_im * lam_im
    q_re = ((a_re - 1.0) * lam_re + a_im * lam_im) / den
    q_im = (a_im * lam_re - (a_re - 1.0) * lam_im) / den
    bb_re = q_re[..., None] * b_re - q_im[..., None] * b_im
    bb_im = q_re[..., None] * b_im + q_im[..., None] * b_re

    def bu_tiles(bb):
        t = jnp.swapaxes(bb, -1, -2).reshape(depth, N_BU_HALF, BU_GROUPS, SSM_GROUP_CH, SSM_STATE)
        t = jnp.einsum('lnqcp,qr->lnqcrp', t, jnp.eye(BU_GROUPS, dtype=_F32))
        t = t.reshape(depth, N_BU_HALF, BU_GROUPS * SSM_GROUP_CH, BU_TILE)
        z = jnp.zeros_like(t)
        first = (jnp.arange(N_BU_HALF) % 2 == 0)[None, :, None, None]
        return jnp.where(first, jnp.concatenate([t, z], axis=2), jnp.concatenate([z, t], axis=2))

    def c_tiles(cp):
        t = cp.reshape(depth, WIDTH_A // C_TILE, C_GROUPS, SSM_GROUP_CH, SSM_STATE)
        t = jnp.einsum('ljgcp,gh->ljgphc', t, jnp.eye(C_GROUPS, dtype=_F32))
        return t.reshape(depth, WIDTH_A // C_TILE, C_K, C_TILE)

    wb = jnp.concatenate([bu_tiles(bb_re), bu_tiles(bb_im)], axis=1).astype(_BF16)
    row = lambda v: v.reshape(depth, 1, -1).astype(_F32)
    g_fin = jnp.broadcast_to(g_final.reshape(1, 1, -1), (depth, 1, D_MODEL)).astype(_F32)
    return (row(g_mix), row(g_ffn), g_fin, w_in.astype(_BF16), wb,
            row(a_re), row(a_im), c_tiles(c_re).astype(_BF16), c_tiles(-c_im).astype(_BF16),
            row(d), w_glu.astype(_BF16), w_pool.astype(_BF16), row(pool_scale),
            w_out.astype(_BF16), w_ff1.astype(_BF16), w_ff2.astype(_BF16))


def _hist_to_time_major(hist):
    return jnp.swapaxes(hist, 0, 1).reshape(-1, hist.shape[-1])


def _hist_from_time_major(hist_tm, nb):
    return jnp.swapaxes(hist_tm.reshape(POOL_HIST, nb, -1), 0, 1)


def kernel(x_prompt, x_sample, state_ssm_re, state_ssm_im, state_pool, g_mix, g_ffn, w_in,
           ssm_lambda_re, ssm_lambda_im, ssm_log_dt, ssm_b_re, ssm_b_im, ssm_c_re, ssm_c_im,
           ssm_d, w_glu, w_pool, pool_scale, w_out, w_ff1, w_ff2, g_final):
    bp = x_prompt.shape[0]
    bs = x_sample.shape[0]
    weights = _prep_weights(g_mix, g_ffn, g_final, w_in, ssm_lambda_re, ssm_lambda_im, ssm_log_dt,
                            ssm_b_re, ssm_b_im, ssm_c_re, ssm_c_im, ssm_d, w_glu, w_pool,
                            pool_scale, w_out, w_ff1, w_ff2)
    xp, xs = x_prompt, x_sample
    zero_state = jnp.zeros((bp, STATE_W), _F32)
    zero_hist = jnp.zeros((POOL_HIST * bp, WIDTH_B), _F32)
    outs = {k: [] for k in ("p_re", "p_im", "p_pool", "s_re", "s_im", "s_pool")}
    for l in range(DEPTH):
        final_norm = l == DEPTH - 1
        xp, h_re, h_im, hist = _layer_call(xp, zero_state, zero_state, zero_hist, weights, l,
                                           start_pos=0, final_norm=final_norm)
        outs["p_re"].append(h_re.reshape(bp, SSM_GROUPS, SSM_STATE))
        outs["p_im"].append(h_im.reshape(bp, SSM_GROUPS, SSM_STATE))
        outs["p_pool"].append(_hist_from_time_major(hist, bp))
        xs, h_re, h_im, hist = _layer_call(
            xs, state_ssm_re[l].reshape(bs, STATE_W).astype(_F32),
            state_ssm_im[l].reshape(bs, STATE_W).astype(_F32),
            _hist_to_time_major(state_pool[l]).astype(_F32), weights, l,
            start_pos=PAST_LEN, final_norm=final_norm)
        outs["s_re"].append(h_re.reshape(bs, SSM_GROUPS, SSM_STATE))
        outs["s_im"].append(h_im.reshape(bs, SSM_GROUPS, SSM_STATE))
        outs["s_pool"].append(_hist_from_time_major(hist, bs))
    return (xp, xs,
            jnp.stack(outs["p_re"]), jnp.stack(outs["p_im"]), jnp.stack(outs["p_pool"]),
            jnp.stack(outs["s_re"]), jnp.stack(outs["s_im"]), jnp.stack(outs["s_pool"]))
```

```python
import functools

import jax
import jax.numpy as jnp
from jax import lax
from jax.experimental import pallas as pl
from jax.experimental.pallas import tpu as pltpu

D_MODEL = 1024
DEPTH = 2
PAST_LEN = 4096
WIDTH_A = 512
WIDTH_B = 512
MIX_WIDTH = WIDTH_A + WIDTH_B
SSM_GROUP_CH = 16
SSM_GROUPS = WIDTH_A // SSM_GROUP_CH
SSM_STATE = 64
STATE_W = SSM_GROUPS * SSM_STATE
POOL_WINDOWS = (2, 4, 8, 16)
POOL_GROUP_CH = WIDTH_B // len(POOL_WINDOWS)
POOL_HIST = max(POOL_WINDOWS) - 1
D_FF = 4 * D_MODEL
RMS_EPS = 1e-5

TILE_ROWS = 512
FF_CHUNK = 1024
BU_TILE = 256
BU_GROUPS = BU_TILE // SSM_STATE
BU_K = 128
N_BU_HALF = STATE_W // BU_TILE
C_TILE = 256
C_GROUPS = C_TILE // SSM_GROUP_CH
C_K = C_GROUPS * SSM_STATE
SCAN_VREG_ELEMS = 4 * 8 * 128
VMEM_LIMIT_BYTES = 60 * 1024 * 1024

_BF16 = jnp.bfloat16
_F32 = jnp.float32


def _rmsnorm(x, g):
    ms = jnp.mean(x * x, axis=-1, keepdims=True)
    return x * lax.rsqrt(ms + RMS_EPS) * g


def _bu_k_block(n):
    return (n % N_BU_HALF) * BU_GROUPS * SSM_GROUP_CH // BU_K


def _layer_kernel(nb, tt, n_tiles, start_pos, final_norm,
                  x_ref, h0re_ref, h0im_ref, hist0_ref,
                  gmix_ref, gffn_ref, gfin_ref, win_ref, wb_ref, are_ref, aim_ref,
                  wcre_ref, wcim_ref, d_ref, wglu_ref, wpool_ref, pscale_ref,
                  wout_ref, wff1_ref, wff2_ref,
                  y_ref, hre_out, him_out, hist_out,
                  hre, him, ext, bu, x1_buf):
    rows = nb * tt
    hist_rows = POOL_HIST * nb
    i = pl.program_id(0)
    live = i < n_tiles

    @pl.when(i == 0)
    def _():
        hre[...] = h0re_ref[...]
        him[...] = h0im_ref[...]
        ext[0:hist_rows, :] = hist0_ref[...]
        x1_buf[...] = jnp.zeros((rows, D_MODEL), _F32)

    x1_prev = x1_buf[...]
    hf = _rmsnorm(x1_prev, gffn_ref[...]).astype(_BF16)

    def mlp_chunk(c, acc):
        mid = jnp.dot(hf, wff1_ref[:, c * FF_CHUNK:(c + 1) * FF_CHUNK], preferred_element_type=_F32)
        mid = jnp.square(jnp.maximum(mid, 0.0)).astype(_BF16)
        return acc + jnp.dot(mid, wff2_ref[c * FF_CHUNK:(c + 1) * FF_CHUNK, :],
                             preferred_element_type=_F32)

    x = x_ref[...].reshape(rows, D_MODEL)
    hn = _rmsnorm(x, gmix_ref[...]).astype(_BF16)
    proj = jnp.dot(hn, win_ref[...], preferred_element_type=_F32)
    proj = jnp.swapaxes(proj.reshape(nb, tt, MIX_WIDTH), 0, 1).reshape(rows, MIX_WIDTH)
    ua = proj[:, :WIDTH_A]
    ub = proj[:, WIDTH_A:]
    ua_bf = ua.astype(_BF16)

    acc = mlp_chunk(0, x1_prev)

    for n in range(2 * N_BU_HALF):
        kb = _bu_k_block(n)
        bu[:, n * BU_TILE:(n + 1) * BU_TILE] = jnp.dot(
            ua_bf[:, kb * BU_K:(kb + 1) * BU_K], wb_ref[n], preferred_element_type=_F32)

    lc = SCAN_VREG_ELEMS // nb
    n_lane_chunks = STATE_W // lc

    def scan_lane_chunk(c):
        re_sl = slice(c * lc, (c + 1) * lc)
        im_sl = slice(STATE_W + c * lc, STATE_W + (c + 1) * lc)
        a_re = jnp.broadcast_to(are_ref[:, re_sl], (nb, lc))
        a_im = jnp.broadcast_to(aim_ref[:, re_sl], (nb, lc))
        h_r0 = hre[:, re_sl]
        h_i0 = him[:, re_sl]
        h_r, h_i = h_r0, h_i0
        for t in range(tt):
            rs = slice(t * nb, (t + 1) * nb)
            n_r = a_re * h_r - a_im * h_i + bu[rs, re_sl]
            n_i = a_re * h_i + a_im * h_r + bu[rs, im_sl]
            bu[rs, re_sl] = n_r
            bu[rs, im_sl] = n_i
            h_r, h_i = n_r, n_i
        hre[:, re_sl] = jnp.where(live, h_r, h_r0)
        him[:, re_sl] = jnp.where(live, h_i, h_i0)

    for c in range(n_lane_chunks // 2):
        scan_lane_chunk(c)

    acc = mlp_chunk(1, acc)

    for c in range(n_lane_chunks // 2, n_lane_chunks):
        scan_lane_chunk(c)
    hre_out[...] = hre[...]
    him_out[...] = him[...]
    y_parts = []
    for j in range(WIDTH_A // C_TILE):
        h_re = bu[:, j * C_K:(j + 1) * C_K].astype(_BF16)
        h_im = bu[:, STATE_W + j * C_K:STATE_W + (j + 1) * C_K].astype(_BF16)
        y_parts.append(jnp.dot(h_re, wcre_ref[j], preferred_element_type=_F32)
                       + jnp.dot(h_im, wcim_ref[j], preferred_element_type=_F32))
    y = jnp.concatenate(y_parts, axis=1) + d_ref[...] * ua
    z = jax.nn.gelu(y)
    out_a = z * jax.nn.sigmoid(jnp.dot(z.astype(_BF16), wglu_ref[...], preferred_element_type=_F32))

    acc = mlp_chunk(2, acc)

    ext[hist_rows:hist_rows + rows, :] = ub
    t0 = start_pos + jnp.minimum(i, n_tiles - 1) * tt
    pos = t0 + lax.broadcasted_iota(jnp.int32, (rows, POOL_GROUP_CH), 0) // nb
    yb_parts = []
    for k, w in enumerate(POOL_WINDOWS):
        ls = slice(k * POOL_GROUP_CH, (k + 1) * POOL_GROUP_CH)
        wsum = ext[hist_rows:hist_rows + rows, ls]
        for j in range(1, w):
            wsum = wsum + ext[(POOL_HIST - j) * nb:(POOL_HIST - j) * nb + rows, ls]
        cnt = jnp.minimum(pos + 1, w).astype(_F32)
        diff = wsum / cnt - ub[:, ls]
        yb_parts.append(jnp.dot(diff.astype(_BF16), wpool_ref[k], preferred_element_type=_F32))
    yb = jnp.concatenate(yb_parts, axis=1) * pscale_ref[...]
    new_hist = jnp.where(live, ext[rows:rows + hist_rows, :], ext[0:hist_rows, :])
    hist_out[...] = new_hist
    ext[0:hist_rows, :] = new_hist

    mixed = jnp.concatenate([out_a, yb], axis=1)
    mixed = jnp.swapaxes(mixed.reshape(tt, nb, MIX_WIDTH), 0, 1).reshape(rows, MIX_WIDTH)
    x1 = x + jnp.dot(mixed.astype(_BF16), wout_ref[...], preferred_element_type=_F32)

    acc = mlp_chunk(3, acc)
    if final_norm:
        acc = _rmsnorm(acc, gfin_ref[...])
    y_ref[...] = acc.reshape(nb, tt, D_MODEL)
    x1_buf[...] = x1


def _layer_call(x, h0_re, h0_im, hist0, weights, layer, *, start_pos, final_norm):
    nb, seq, _ = x.shape
    tt = TILE_ROWS // nb
    assert seq % tt == 0 and tt >= POOL_HIST and tt % 8 == 0
    assert D_FF // FF_CHUNK == 4, "the kernel body alternates four MLP chunks with four mixer pieces"
    n_tiles = seq // tt
    hist_rows = POOL_HIST * nb

    def resident(w):
        zeros = (0,) * (w.ndim - 1)
        return pl.BlockSpec((None,) + w.shape[1:], lambda i: (layer,) + zeros,
                            pipeline_mode=pl.Buffered(1))

    whole = pl.BlockSpec(memory_space=pltpu.VMEM)
    tile_shape = (nb, tt, D_MODEL)
    x_mixer = pl.BlockSpec(tile_shape, lambda i: (0, jnp.minimum(i, n_tiles - 1), 0))
    y_mlp = pl.BlockSpec(tile_shape, lambda i: (0, jnp.maximum(i - 1, 0), 0))
    out_shape = (
        jax.ShapeDtypeStruct(x.shape, _F32),
        jax.ShapeDtypeStruct((nb, STATE_W), _F32),
        jax.ShapeDtypeStruct((nb, STATE_W), _F32),
        jax.ShapeDtypeStruct((hist_rows, WIDTH_B), _F32),
    )
    return pl.pallas_call(
        functools.partial(_layer_kernel, nb, tt, n_tiles, start_pos, final_norm),
        grid=(n_tiles + 1,),
        in_specs=[x_mixer, whole, whole, whole] + [resident(w) for w in weights],
        out_specs=(y_mlp, whole, whole, whole),
        out_shape=out_shape,
        scratch_shapes=[
            pltpu.VMEM((nb, STATE_W), _F32),
            pltpu.VMEM((nb, STATE_W), _F32),
            pltpu.VMEM((hist_rows + TILE_ROWS, WIDTH_B), _F32),
            pltpu.VMEM((TILE_ROWS, 2 * STATE_W), _F32),
            pltpu.VMEM((TILE_ROWS, D_MODEL), _F32),
        ],
        compiler_params=pltpu.CompilerParams(
            dimension_semantics=("arbitrary",), vmem_limit_bytes=VMEM_LIMIT_BYTES),
    )(x, h0_re, h0_im, hist0, *weights)


def _prep_weights(g_mix, g_ffn, g_final, w_in, lam_re, lam_im, log_dt, b_re, b_im, c_re, c_im,
                  d, w_glu, w_pool, pool_scale, w_out, w_ff1, w_ff2):
    depth = w_in.shape[0]
    dt = jnp.exp(log_dt)[..., None]
    mag = jnp.exp(lam_re * dt)
    a_re = mag * jnp.cos(lam_im * dt)
    a_im = mag * jnp.sin(lam_im * dt)
    den = lam_re * lam_re + lam_im * lam_im
    q_re = ((a_re - 1.0) * lam_re + a_im * lam_im) / den
    q_im = (a_im * lam_re - (a_re - 1.0) * lam_im) / den
    bb_re = q_re[..., None] * b_re - q_im[..., None] * b_im
    bb_im = q_re[..., None] * b_im + q_im[..., None] * b_re

    def bu_tiles(bb):
        t = jnp.swapaxes(bb, -1, -2).reshape(depth, N_BU_HALF, BU_GROUPS, SSM_GROUP_CH, SSM_STATE)
        t = jnp.einsum('lnqcp,qr->lnqcrp', t, jnp.eye(BU_GROUPS, dtype=_F32))
        t = t.reshape(depth, N_BU_HALF, BU_GROUPS * SSM_GROUP_CH, BU_TILE)
        z = jnp.zeros_like(t)
        first = (jnp.arange(N_BU_HALF) % 2 == 0)[None, :, None, None]
        return jnp.where(first, jnp.concatenate([t, z], axis=2), jnp.concatenate([z, t], axis=2))

    def c_tiles(cp):
        t = cp.reshape(depth, WIDTH_A // C_TILE, C_GROUPS, SSM_GROUP_CH, SSM_STATE)
        t = jnp.einsum('ljgcp,gh->ljgphc', t, jnp.eye(C_GROUPS, dtype=_F32))
        return t.reshape(depth, WIDTH_A // C_TILE, C_K, C_TILE)

    wb = jnp.concatenate([bu_tiles(bb_re), bu_tiles(bb_im)], axis=1).astype(_BF16)
    row = lambda v: v.reshape(depth, 1, -1).astype(_F32)
    g_fin = jnp.broadcast_to(g_final.reshape(1, 1, -1), (depth, 1, D_MODEL)).astype(_F32)
    return (row(g_mix), row(g_ffn), g_fin, w_in.astype(_BF16), wb,
            row(a_re), row(a_im), c_tiles(c_re).astype(_BF16), c_tiles(-c_im).astype(_BF16),
            row(d), w_glu.astype(_BF16), w_pool.astype(_BF16), row(pool_scale),
            w_out.astype(_BF16), w_ff1.astype(_BF16), w_ff2.astype(_BF16))


def _hist_to_time_major(hist):
    return jnp.swapaxes(hist, 0, 1).reshape(-1, hist.shape[-1])


def _hist_from_time_major(hist_tm, nb):
    return jnp.swapaxes(hist_tm.reshape(POOL_HIST, nb, -1), 0, 1)


def kernel(x_prompt, x_sample, state_ssm_re, state_ssm_im, state_pool, g_mix, g_ffn, w_in,
           ssm_lambda_re, ssm_lambda_im, ssm_log_dt, ssm_b_re, ssm_b_im, ssm_c_re, ssm_c_im,
           ssm_d, w_glu, w_pool, pool_scale, w_out, w_ff1, w_ff2, g_final):
    bp = x_prompt.shape[0]
    bs = x_sample.shape[0]
    weights = _prep_weights(g_mix, g_ffn, g_final, w_in, ssm_lambda_re, ssm_lambda_im, ssm_log_dt,
                            ssm_b_re, ssm_b_im, ssm_c_re, ssm_c_im, ssm_d, w_glu, w_pool,
                            pool_scale, w_out, w_ff1, w_ff2)
    xp, xs = x_prompt, x_sample
    zero_state = jnp.zeros((bp, STATE_W), _F32)
    zero_hist = jnp.zeros((POOL_HIST * bp, WIDTH_B), _F32)
    outs = {k: [] for k in ("p_re", "p_im", "p_pool", "s_re", "s_im", "s_pool")}
    for l in range(DEPTH):
        final_norm = l == DEPTH - 1
        xp, h_re, h_im, hist = _layer_call(xp, zero_state, zero_state, zero_hist, weights, l,
                                           start_pos=0, final_norm=final_norm)
        outs["p_re"].append(h_re.reshape(bp, SSM_GROUPS, SSM_STATE))
        outs["p_im"].append(h_im.reshape(bp, SSM_GROUPS, SSM_STATE))
        outs["p_pool"].append(_hist_from_time_major(hist, bp))
        xs, h_re, h_im, hist = _layer_call(
            xs, state_ssm_re[l].reshape(bs, STATE_W).astype(_F32),
            state_ssm_im[l].reshape(bs, STATE_W).astype(_F32),
            _hist_to_time_major(state_pool[l]).astype(_F32), weights, l,
            start_pos=PAST_LEN, final_norm=final_norm)
        outs["s_re"].append(h_re.reshape(bs, SSM_GROUPS, SSM_STATE))
        outs["s_im"].append(h_im.reshape(bs, SSM_GROUPS, SSM_STATE))
        outs["s_pool"].append(_hist_from_time_major(hist, bs))
    return (xp, xs,
            jnp.stack(outs["p_re"]), jnp.stack(outs["p_im"]), jnp.stack(outs["p_pool"]),
            jnp.stack(outs["s_re"]), jnp.stack(outs["s_im"]), jnp.stack(outs["s_pool"]))
```

```python
import functools

import jax
import jax.numpy as jnp
from jax import lax
from jax.experimental import pallas as pl
from jax.experimental.pallas import tpu as pltpu

D_MODEL = 1024
DEPTH = 2
PAST_LEN = 4096
WIDTH_A = 512
WIDTH_B = 512
MIX_WIDTH = WIDTH_A + WIDTH_B
SSM_GROUP_CH = 16
SSM_GROUPS = WIDTH_A // SSM_GROUP_CH
SSM_STATE = 64
STATE_W = SSM_GROUPS * SSM_STATE
POOL_WINDOWS = (2, 4, 8, 16)
POOL_GROUP_CH = WIDTH_B // len(POOL_WINDOWS)
POOL_HIST = max(POOL_WINDOWS) - 1
D_FF = 4 * D_MODEL
RMS_EPS = 1e-5

TILE_ROWS = 512
FF_CHUNK = 1024
MXU_TILE = 256
BU_TILE = MXU_TILE
BU_GROUPS = BU_TILE // SSM_STATE
BU_K = 128
N_BU_HALF = STATE_W // BU_TILE
RD_GROUPS = 8
RD_K = RD_GROUPS * SSM_STATE
RD_N = RD_GROUPS * SSM_GROUP_CH
N_RD = SSM_GROUPS // RD_GROUPS
SCAN_LANES = 128
VMEM_LIMIT_BYTES = 60 * 1024 * 1024

_BF16 = jnp.bfloat16
_F32 = jnp.float32


def _rmsnorm(x, g):
    ms = jnp.mean(x * x, axis=-1, keepdims=True)
    return x * lax.rsqrt(ms + RMS_EPS) * g


def _bu_k_block(n):
    return (n % N_BU_HALF) * BU_GROUPS * SSM_GROUP_CH // BU_K


def _layer_kernel(nb, tt, n_tiles, start_pos, final_norm,
                  x_ref, h0re_ref, h0im_ref, hist0_ref,
                  gmix_ref, gffn_ref, gfin_ref, win_ref, wp_ref, a2re_ref, a2im_ref,
                  wrre_ref, wrim_ref, wd_ref, d_ref, wglu_ref, wpool_ref, pscale_ref,
                  wout_ref, wff1_ref, wff2_ref,
                  y_ref, hre_out, him_out, hist_out,
                  hre, him, ext, hs, x1_buf, hf_buf):
    rows = nb * tt
    half = rows // 2
    hist_rows = POOL_HIST * nb
    i = pl.program_id(0)
    live = i < n_tiles

    @pl.when(i == 0)
    def _():
        hre[...] = h0re_ref[...]
        him[...] = h0im_ref[...]
        ext[0:hist_rows, :] = hist0_ref[...]
        x1_buf[...] = jnp.zeros((rows, D_MODEL), _F32)
        hf_buf[...] = jnp.zeros((rows, D_MODEL), _BF16)

    x1_prev = x1_buf[...]
    hf = hf_buf[...]

    def mlp_chunk(c, acc):
        mid = jnp.dot(hf, wff1_ref[:, c * FF_CHUNK:(c + 1) * FF_CHUNK], preferred_element_type=_F32)
        mid = jnp.square(jnp.maximum(mid, 0.0)).astype(_BF16)
        return acc + jnp.dot(mid, wff2_ref[c * FF_CHUNK:(c + 1) * FF_CHUNK, :],
                             preferred_element_type=_F32)

    acc = mlp_chunk(0, x1_prev)

    x = x_ref[...].reshape(rows, D_MODEL)
    hn = _rmsnorm(x, gmix_ref[...]).astype(_BF16)
    proj = jnp.dot(hn, win_ref[...], preferred_element_type=_F32)
    proj = jnp.swapaxes(proj.reshape(nb, tt, MIX_WIDTH), 0, 1).reshape(rows, MIX_WIDTH)
    ua = proj[:, :WIDTH_A]
    ub = proj[:, WIDTH_A:]
    ua_pairs = ua.reshape(tt // 2, 2, nb, WIDTH_A)
    ue_bf = ua_pairs[:, 0].reshape(half, WIDTH_A).astype(_BF16)
    uo_bf = ua_pairs[:, 1].reshape(half, WIDTH_A).astype(_BF16)

    for n in range(2 * N_BU_HALF):
        ks = slice(_bu_k_block(n) * BU_K, (_bu_k_block(n) + 1) * BU_K)
        lhs = jnp.concatenate([ue_bf[:, ks], uo_bf[:, ks]], axis=1)
        hs[nb:nb + half, n * BU_TILE:(n + 1) * BU_TILE] = jnp.dot(
            lhs, wp_ref[n], preferred_element_type=_F32)

    lc = SCAN_LANES
    n_lane_chunks = STATE_W // lc
    always = i >= 0

    def scan_lane_chunk(c, chain):
        re_sl = slice(c * lc, (c + 1) * lc)
        im_sl = slice(STATE_W + c * lc, STATE_W + (c + 1) * lc)
        a_re = jnp.broadcast_to(a2re_ref[:, re_sl], (nb, lc))
        a_im = jnp.broadcast_to(a2im_ref[:, re_sl], (nb, lc))
        h_r0 = hre[:, re_sl]
        h_i0 = him[:, re_sl]
        hs[0:nb, re_sl] = h_r0
        hs[0:nb, im_sl] = h_i0
        h_r, h_i = h_r0, h_i0
        if chain is not None:
            h_r = jnp.where(always, h_r, chain)
        for k in range(tt // 2):
            rs = slice((k + 1) * nb, (k + 2) * nb)
            n_r = a_re * h_r - a_im * h_i + hs[rs, re_sl]
            n_i = a_re * h_i + a_im * h_r + hs[rs, im_sl]
            hs[rs, re_sl] = n_r
            hs[rs, im_sl] = n_i
            h_r, h_i = n_r, n_i
        hre[:, re_sl] = jnp.where(live, h_r, h_r0)
        him[:, re_sl] = jnp.where(live, h_i, h_i0)
        return h_r

    chain = None
    for c in range(n_lane_chunks // 2):
        chain = scan_lane_chunk(c, chain)

    acc = mlp_chunk(1, acc)

    for c in range(n_lane_chunks // 2, n_lane_chunks):
        chain = scan_lane_chunk(c, chain)
    hre_out[...] = hre[...]
    him_out[...] = him[...]
    y_odd, y_even = [], []
    for m in range(N_RD):
        h_re = hs[:, m * RD_K:(m + 1) * RD_K].astype(_BF16)
        h_im = hs[:, STATE_W + m * RD_K:STATE_W + (m + 1) * RD_K].astype(_BF16)
        out = (jnp.dot(h_re, wrre_ref[m], preferred_element_type=_F32)
               + jnp.dot(h_im, wrim_ref[m], preferred_element_type=_F32))
        y_odd.append(out[nb:, :RD_N])
        y_even.append(out[:half, RD_N:])
    direct = [jnp.dot(ue_bf[:, m * MXU_TILE:(m + 1) * MXU_TILE], wd_ref[m],
                      preferred_element_type=_F32) for m in range(WIDTH_A // MXU_TILE)]
    y_even = jnp.concatenate(y_even, axis=1) + jnp.concatenate(direct, axis=1)
    y_odd = jnp.concatenate(y_odd, axis=1)
    ys = jnp.stack([y_even.reshape(tt // 2, nb, WIDTH_A), y_odd.reshape(tt // 2, nb, WIDTH_A)],
                   axis=1).reshape(rows, WIDTH_A)
    y = ys + d_ref[...] * ua
    z = jax.nn.gelu(y)
    out_a = z * jax.nn.sigmoid(jnp.dot(z.astype(_BF16), wglu_ref[...], preferred_element_type=_F32))

    acc = mlp_chunk(2, acc)

    ext[hist_rows:hist_rows + rows, :] = ub
    t0 = start_pos + jnp.minimum(i, n_tiles - 1) * tt
    pos = t0 + lax.broadcasted_iota(jnp.int32, (rows, POOL_GROUP_CH), 0) // nb
    diffs = []
    for k, w in enumerate(POOL_WINDOWS):
        ls = slice(k * POOL_GROUP_CH, (k + 1) * POOL_GROUP_CH)
        wsum = ext[hist_rows:hist_rows + rows, ls]
        for j in range(1, w):
            wsum = wsum + ext[(POOL_HIST - j) * nb:(POOL_HIST - j) * nb + rows, ls]
        cnt = jnp.minimum(pos + 1, w).astype(_F32)
        diffs.append((wsum / cnt - ub[:, ls]).astype(_BF16))
    yb_parts = [jnp.dot(jnp.concatenate(diffs[2 * m:2 * m + 2], axis=1), wpool_ref[m],
                        preferred_element_type=_F32) for m in range(len(POOL_WINDOWS) // 2)]
    yb = jnp.concatenate(yb_parts, axis=1) * pscale_ref[...]
    new_hist = jnp.where(live, ext[rows:rows + hist_rows, :], ext[0:hist_rows, :])
    hist_out[...] = new_hist
    ext[0:hist_rows, :] = new_hist

    mixed = jnp.concatenate([out_a, yb], axis=1)
    mixed = jnp.swapaxes(mixed.reshape(tt, nb, MIX_WIDTH), 0, 1).reshape(rows, MIX_WIDTH)
    x1 = x + jnp.dot(mixed.astype(_BF16), wout_ref[...], preferred_element_type=_F32)

    acc = mlp_chunk(3, acc)
    if final_norm:
        acc = _rmsnorm(acc, gfin_ref[...])
    y_ref[...] = acc.reshape(nb, tt, D_MODEL)
    x1_buf[...] = x1
    hf_buf[...] = _rmsnorm(x1, gffn_ref[...]).astype(_BF16)


def _layer_call(x, h0_re, h0_im, hist0, weights, layer, *, start_pos, final_norm):
    nb, seq, _ = x.shape
    tt = TILE_ROWS // nb
    assert seq % tt == 0 and tt >= POOL_HIST and tt % 8 == 0
    assert D_FF // FF_CHUNK == 4, "the kernel body alternates four MLP chunks with four mixer pieces"
    n_tiles = seq // tt
    hist_rows = POOL_HIST * nb

    def resident(w):
        zeros = (0,) * (w.ndim - 1)
        return pl.BlockSpec((None,) + w.shape[1:], lambda i: (layer,) + zeros,
                            pipeline_mode=pl.Buffered(1))

    whole = pl.BlockSpec(memory_space=pltpu.VMEM)
    tile_shape = (nb, tt, D_MODEL)
    x_mixer = pl.BlockSpec(tile_shape, lambda i: (0, jnp.minimum(i, n_tiles - 1), 0))
    y_mlp = pl.BlockSpec(tile_shape, lambda i: (0, jnp.maximum(i - 1, 0), 0))
    out_shape = (
        jax.ShapeDtypeStruct(x.shape, _F32),
        jax.ShapeDtypeStruct((nb, STATE_W), _F32),
        jax.ShapeDtypeStruct((nb, STATE_W), _F32),
        jax.ShapeDtypeStruct((hist_rows, WIDTH_B), _F32),
    )
    return pl.pallas_call(
        functools.partial(_layer_kernel, nb, tt, n_tiles, start_pos, final_norm),
        grid=(n_tiles + 1,),
        in_specs=[x_mixer, whole, whole, whole] + [resident(w) for w in weights],
        out_specs=(y_mlp, whole, whole, whole),
        out_shape=out_shape,
        scratch_shapes=[
            pltpu.VMEM((nb, STATE_W), _F32),
            pltpu.VMEM((nb, STATE_W), _F32),
            pltpu.VMEM((hist_rows + TILE_ROWS, WIDTH_B), _F32),
            pltpu.VMEM((TILE_ROWS // 2 + nb, 2 * STATE_W), _F32),
            pltpu.VMEM((TILE_ROWS, D_MODEL), _F32),
            pltpu.VMEM((TILE_ROWS, D_MODEL), _BF16),
        ],
        compiler_params=pltpu.CompilerParams(
            dimension_semantics=("arbitrary",), vmem_limit_bytes=VMEM_LIMIT_BYTES),
    )(x, h0_re, h0_im, hist0, *weights)


def _prep_weights(g_mix, g_ffn, g_final, w_in, lam_re, lam_im, log_dt, b_re, b_im, c_re, c_im,
                  d, w_glu, w_pool, pool_scale, w_out, w_ff1, w_ff2):
    depth = w_in.shape[0]
    dt = jnp.exp(log_dt)[..., None]
    mag = jnp.exp(lam_re * dt)
    a_re = mag * jnp.cos(lam_im * dt)
    a_im = mag * jnp.sin(lam_im * dt)
    den = lam_re * lam_re + lam_im * lam_im
    q_re = ((a_re - 1.0) * lam_re + a_im * lam_im) / den
    q_im = (a_im * lam_re - (a_re - 1.0) * lam_im) / den
    bb_re = q_re[..., None] * b_re - q_im[..., None] * b_im
    bb_im = q_re[..., None] * b_im + q_im[..., None] * b_re
    ab_re = a_re[..., None] * bb_re - a_im[..., None] * bb_im
    ab_im = a_re[..., None] * bb_im + a_im[..., None] * bb_re
    a2_re = a_re * a_re - a_im * a_im
    a2_im = 2.0 * a_re * a_im
    ca_re = c_re * a_re[:, :, None, :] - c_im * a_im[:, :, None, :]
    ca_im = c_re * a_im[:, :, None, :] + c_im * a_re[:, :, None, :]
    direct = (jnp.einsum('lgcp,lgpd->lgcd', c_re, bb_re)
              - jnp.einsum('lgcp,lgpd->lgcd', c_im, bb_im))

    def bu_tiles(bb):
        t = jnp.swapaxes(bb, -1, -2).reshape(depth, N_BU_HALF, BU_GROUPS, SSM_GROUP_CH, SSM_STATE)
        t = jnp.einsum('lnqcp,qr->lnqcrp', t, jnp.eye(BU_GROUPS, dtype=_F32))
        t = t.reshape(depth, N_BU_HALF, BU_GROUPS * SSM_GROUP_CH, BU_TILE)
        z = jnp.zeros_like(t)
        first = (jnp.arange(N_BU_HALF) % 2 == 0)[None, :, None, None]
        return jnp.where(first, jnp.concatenate([t, z], axis=2), jnp.concatenate([z, t], axis=2))

    def pair_tiles(ab, bb):
        return jnp.concatenate([bu_tiles(ab), bu_tiles(bb)], axis=2)

    def rd_block(cp):
        t = cp.reshape(depth, N_RD, RD_GROUPS, SSM_GROUP_CH, SSM_STATE)
        t = jnp.einsum('lmgcp,gh->lmgphc', t, jnp.eye(RD_GROUPS, dtype=_F32))
        return t.reshape(depth, N_RD, RD_K, RD_N)

    def rd_tiles(cp, cap):
        return jnp.concatenate([rd_block(cp), rd_block(cap)], axis=-1)

    n_dt = WIDTH_A // MXU_TILE
    g_dt = MXU_TILE // SSM_GROUP_CH
    wd = jnp.einsum('lmgcd,gh->lmgdhc', direct.reshape(depth, n_dt, g_dt, SSM_GROUP_CH, SSM_GROUP_CH),
                    jnp.eye(g_dt, dtype=_F32)).reshape(depth, n_dt, MXU_TILE, MXU_TILE)
    n_pp = len(POOL_WINDOWS) // 2
    wpool2 = jnp.einsum('lmkcd,kj->lmkcjd', w_pool.reshape(depth, n_pp, 2, POOL_GROUP_CH, POOL_GROUP_CH),
                        jnp.eye(2, dtype=_F32)).reshape(depth, n_pp, 2 * POOL_GROUP_CH, 2 * POOL_GROUP_CH)
    wp = jnp.concatenate([pair_tiles(ab_re, bb_re), pair_tiles(ab_im, bb_im)], axis=1)
    row = lambda v: v.reshape(depth, 1, -1).astype(_F32)
    g_fin = jnp.broadcast_to(g_final.reshape(1, 1, -1), (depth, 1, D_MODEL)).astype(_F32)
    return (row(g_mix), row(g_ffn), g_fin, w_in.astype(_BF16), wp.astype(_BF16),
            row(a2_re), row(a2_im), rd_tiles(c_re, ca_re).astype(_BF16),
            rd_tiles(-c_im, -ca_im).astype(_BF16), wd.astype(_BF16),
            row(d), w_glu.astype(_BF16), wpool2.astype(_BF16), row(pool_scale),
            w_out.astype(_BF16), w_ff1.astype(_BF16), w_ff2.astype(_BF16))


def _hist_to_time_major(hist):
    return jnp.swapaxes(hist, 0, 1).reshape(-1, hist.shape[-1])


def _hist_from_time_major(hist_tm, nb):
    return jnp.swapaxes(hist_tm.reshape(POOL_HIST, nb, -1), 0, 1)


def kernel(x_prompt, x_sample, state_ssm_re, state_ssm_im, state_pool, g_mix, g_ffn, w_in,
           ssm_lambda_re, ssm_lambda_im, ssm_log_dt, ssm_b_re, ssm_b_im, ssm_c_re, ssm_c_im,
           ssm_d, w_glu, w_pool, pool_scale, w_out, w_ff1, w_ff2, g_final):
    bp = x_prompt.shape[0]
    bs = x_sample.shape[0]
    weights = _prep_weights(g_mix, g_ffn, g_final, w_in, ssm_lambda_re, ssm_lambda_im, ssm_log_dt,
                            ssm_b_re, ssm_b_im, ssm_c_re, ssm_c_im, ssm_d, w_glu, w_pool,
                            pool_scale, w_out, w_ff1, w_ff2)
    xp, xs = x_prompt, x_sample
    zero_state = jnp.zeros((bp, STATE_W), _F32)
    zero_hist = jnp.zeros((POOL_HIST * bp, WIDTH_B), _F32)
    outs = {k: [] for k in ("p_re", "p_im", "p_pool", "s_re", "s_im", "s_pool")}
    for l in range(DEPTH):
        final_norm = l == DEPTH - 1
        xp, h_re, h_im, hist = _layer_call(xp, zero_state, zero_state, zero_hist, weights, l,
                                           start_pos=0, final_norm=final_norm)
        outs["p_re"].append(h_re.reshape(bp, SSM_GROUPS, SSM_STATE))
        outs["p_im"].append(h_im.reshape(bp, SSM_GROUPS, SSM_STATE))
        outs["p_pool"].append(_hist_from_time_major(hist, bp))
        xs, h_re, h_im, hist = _layer_call(
            xs, state_ssm_re[l].reshape(bs, STATE_W).astype(_F32),
            state_ssm_im[l].reshape(bs, STATE_W).astype(_F32),
            _hist_to_time_major(state_pool[l]).astype(_F32), weights, l,
            start_pos=PAST_LEN, final_norm=final_norm)
        outs["s_re"].append(h_re.reshape(bs, SSM_GROUPS, SSM_STATE))
        outs["s_im"].append(h_im.reshape(bs, SSM_GROUPS, SSM_STATE))
        outs["s_pool"].append(_hist_from_time_major(hist, bs))
    return (xp, xs,
            jnp.stack(outs["p_re"]), jnp.stack(outs["p_im"]), jnp.stack(outs["p_pool"]),
            jnp.stack(outs["s_re"]), jnp.stack(outs["s_im"]), jnp.stack(outs["s_pool"]))
```

```python
import functools

import jax
import jax.numpy as jnp
from jax import lax
from jax.experimental import pallas as pl
from jax.experimental.pallas import tpu as pltpu

D_MODEL = 1024
DEPTH = 2
PAST_LEN = 4096
WIDTH_A = 512
WIDTH_B = 512
MIX_WIDTH = WIDTH_A + WIDTH_B
SSM_GROUP_CH = 16
SSM_GROUPS = WIDTH_A // SSM_GROUP_CH
SSM_STATE = 64
STATE_W = SSM_GROUPS * SSM_STATE
POOL_WINDOWS = (2, 4, 8, 16)
POOL_GROUP_CH = WIDTH_B // len(POOL_WINDOWS)
POOL_HIST = max(POOL_WINDOWS) - 1
D_FF = 4 * D_MODEL
RMS_EPS = 1e-5

TILE_ROWS = 512
FF_CHUNK = 1024
MXU_TILE = 256
BU_TILE = MXU_TILE
BU_GROUPS = BU_TILE // SSM_STATE
BU_K = 128
N_BU_HALF = STATE_W // BU_TILE
RD_GROUPS = 8
RD_K = RD_GROUPS * SSM_STATE
RD_N = RD_GROUPS * SSM_GROUP_CH
N_RD = SSM_GROUPS // RD_GROUPS
SCAN_LANES = 128
VMEM_LIMIT_BYTES = 60 * 1024 * 1024

_BF16 = jnp.bfloat16
_F32 = jnp.float32


def _rmsnorm(x, g):
    ms = jnp.mean(x * x, axis=-1, keepdims=True)
    return x * lax.rsqrt(ms + RMS_EPS) * g


def _bu_k_block(n):
    return (n % N_BU_HALF) * BU_GROUPS * SSM_GROUP_CH // BU_K


def _layer_kernel(nb, tt, n_tiles, start_pos, final_norm,
                  x_hbm, h0re_ref, h0im_ref, hist0_ref,
                  gmix_ref, gffn_ref, gfin_ref, win_ref, wp_ref, a2re_ref, a2im_ref,
                  wrre_ref, wrim_ref, wd_ref, d_ref, wglu_ref, wpool_ref, pscale_ref,
                  wout_ref, wff1_ref, wff2_ref,
                  y_hbm, hre_out, him_out, hist_out,
                  hre, him, ext, hs, x1_buf, hf_buf, xbuf, ybuf, sem_in, sem_out):
    rows = nb * tt
    half = rows // 2
    hist_rows = POOL_HIST * nb
    i = pl.program_id(0)
    live = i < n_tiles

    slot = i % 2
    yslot = 1 - slot

    def x_copies(tile, s):
        t0 = pl.multiple_of(tile * tt, tt)
        return [pltpu.make_async_copy(x_hbm.at[b, pl.ds(t0, tt), :], xbuf.at[s, :, b, :], sem_in.at[s])
                for b in range(nb)]

    def y_copies(tile, s):
        t0 = pl.multiple_of(tile * tt, tt)
        return [pltpu.make_async_copy(ybuf.at[s, :, b, :], y_hbm.at[b, pl.ds(t0, tt), :], sem_out.at[s])
                for b in range(nb)]

    @pl.when(i == 0)
    def _():
        for c in x_copies(0, 0):
            c.start()
        hre[...] = h0re_ref[...]
        him[...] = h0im_ref[...]
        ext[0:hist_rows, :] = hist0_ref[...]
        x1_buf[...] = jnp.zeros((rows, D_MODEL), _F32)
        hf_buf[...] = jnp.zeros((rows, D_MODEL), _BF16)

    @pl.when(i + 1 < n_tiles)
    def _():
        for c in x_copies(i + 1, yslot):
            c.start()

    @pl.when(i < n_tiles)
    def _():
        for c in x_copies(i, slot):
            c.wait()

    @pl.when(i >= 3)
    def _():
        for c in y_copies(i - 3, yslot):
            c.wait()

    x1_prev = x1_buf[...]
    hf = hf_buf[...]

    def mlp_chunk(c, acc):
        mid = jnp.dot(hf, wff1_ref[:, c * FF_CHUNK:(c + 1) * FF_CHUNK], preferred_element_type=_F32)
        mid = jnp.square(jnp.maximum(mid, 0.0)).astype(_BF16)
        return acc + jnp.dot(mid, wff2_ref[c * FF_CHUNK:(c + 1) * FF_CHUNK, :],
                             preferred_element_type=_F32)

    acc = mlp_chunk(0, x1_prev)

    x = xbuf[slot].reshape(rows, D_MODEL)
    hn = _rmsnorm(x, gmix_ref[...]).astype(_BF16)
    proj = jnp.dot(hn, win_ref[...], preferred_element_type=_F32)
    ua = proj[:, :WIDTH_A]
    ub = proj[:, WIDTH_A:]
    ua_pairs = ua.reshape(tt // 2, 2, nb, WIDTH_A)
    ue_bf = ua_pairs[:, 0].reshape(half, WIDTH_A).astype(_BF16)
    uo_bf = ua_pairs[:, 1].reshape(half, WIDTH_A).astype(_BF16)

    for n in range(2 * N_BU_HALF):
        ks = slice(_bu_k_block(n) * BU_K, (_bu_k_block(n) + 1) * BU_K)
        lhs = jnp.concatenate([ue_bf[:, ks], uo_bf[:, ks]], axis=1)
        hs[nb:nb + half, n * BU_TILE:(n + 1) * BU_TILE] = jnp.dot(
            lhs, wp_ref[n], preferred_element_type=_F32)

    lc = SCAN_LANES
    n_lane_chunks = STATE_W // lc
    always = i >= 0

    def scan_lane_chunk(c, chain):
        re_sl = slice(c * lc, (c + 1) * lc)
        im_sl = slice(STATE_W + c * lc, STATE_W + (c + 1) * lc)
        a_re = jnp.broadcast_to(a2re_ref[:, re_sl], (nb, lc))
        a_im = jnp.broadcast_to(a2im_ref[:, re_sl], (nb, lc))
        h_r0 = hre[:, re_sl]
        h_i0 = him[:, re_sl]
        hs[0:nb, re_sl] = h_r0
        hs[0:nb, im_sl] = h_i0
        h_r, h_i = h_r0, h_i0
        if chain is not None:
            h_r = jnp.where(always, h_r, chain)
        for k in range(tt // 2):
            rs = slice((k + 1) * nb, (k + 2) * nb)
            n_r = a_re * h_r - a_im * h_i + hs[rs, re_sl]
            n_i = a_re * h_i + a_im * h_r + hs[rs, im_sl]
            hs[rs, re_sl] = n_r
            hs[rs, im_sl] = n_i
            h_r, h_i = n_r, n_i
        hre[:, re_sl] = jnp.where(live, h_r, h_r0)
        him[:, re_sl] = jnp.where(live, h_i, h_i0)
        return h_r

    chain = None
    for c in range(n_lane_chunks // 2):
        chain = scan_lane_chunk(c, chain)

    acc = mlp_chunk(1, acc)

    for c in range(n_lane_chunks // 2, n_lane_chunks):
        chain = scan_lane_chunk(c, chain)
    hre_out[...] = hre[...]
    him_out[...] = him[...]
    y_odd, y_even = [], []
    for m in range(N_RD):
        h_re = hs[:, m * RD_K:(m + 1) * RD_K].astype(_BF16)
        h_im = hs[:, STATE_W + m * RD_K:STATE_W + (m + 1) * RD_K].astype(_BF16)
        out = (jnp.dot(h_re, wrre_ref[m], preferred_element_type=_F32)
               + jnp.dot(h_im, wrim_ref[m], preferred_element_type=_F32))
        y_odd.append(out[nb:, :RD_N])
        y_even.append(out[:half, RD_N:])
    direct = [jnp.dot(ue_bf[:, m * MXU_TILE:(m + 1) * MXU_TILE], wd_ref[m],
                      preferred_element_type=_F32) for m in range(WIDTH_A // MXU_TILE)]
    y_even = jnp.concatenate(y_even, axis=1) + jnp.concatenate(direct, axis=1)
    y_odd = jnp.concatenate(y_odd, axis=1)
    ys = jnp.stack([y_even.reshape(tt // 2, nb, WIDTH_A), y_odd.reshape(tt // 2, nb, WIDTH_A)],
                   axis=1).reshape(rows, WIDTH_A)
    y = ys + d_ref[...] * ua
    z = jax.nn.gelu(y)
    out_a = z * jax.nn.sigmoid(jnp.dot(z.astype(_BF16), wglu_ref[...], preferred_element_type=_F32))

    acc = mlp_chunk(2, acc)

    ext[hist_rows:hist_rows + rows, :] = ub
    t0 = start_pos + jnp.minimum(i, n_tiles - 1) * tt
    pos = t0 + lax.broadcasted_iota(jnp.int32, (rows, POOL_GROUP_CH), 0) // nb
    diffs = []
    for k, w in enumerate(POOL_WINDOWS):
        ls = slice(k * POOL_GROUP_CH, (k + 1) * POOL_GROUP_CH)
        wsum = ext[hist_rows:hist_rows + rows, ls]
        for j in range(1, w):
            wsum = wsum + ext[(POOL_HIST - j) * nb:(POOL_HIST - j) * nb + rows, ls]
        cnt = jnp.minimum(pos + 1, w).astype(_F32)
        diffs.append((wsum / cnt - ub[:, ls]).astype(_BF16))
    yb_parts = [jnp.dot(jnp.concatenate(diffs[2 * m:2 * m + 2], axis=1), wpool_ref[m],
                        preferred_element_type=_F32) for m in range(len(POOL_WINDOWS) // 2)]
    yb = jnp.concatenate(yb_parts, axis=1) * pscale_ref[...]
    new_hist = jnp.where(live, ext[rows:rows + hist_rows, :], ext[0:hist_rows, :])
    hist_out[...] = new_hist
    ext[0:hist_rows, :] = new_hist

    mixed = jnp.concatenate([out_a, yb], axis=1)
    x1 = x + jnp.dot(mixed.astype(_BF16), wout_ref[...], preferred_element_type=_F32)

    acc = mlp_chunk(3, acc)
    if final_norm:
        acc = _rmsnorm(acc, gfin_ref[...])
    ybuf[yslot] = acc.reshape(tt, nb, D_MODEL)
    x1_buf[...] = x1
    hf_buf[...] = _rmsnorm(x1, gffn_ref[...]).astype(_BF16)

    @pl.when(i >= 1)
    def _():
        for c in y_copies(i - 1, yslot):
            c.start()

    @pl.when(i == n_tiles)
    def _():
        for c in y_copies(i - 2, slot) + y_copies(i - 1, yslot):
            c.wait()


def _layer_call(x, h0_re, h0_im, hist0, weights, layer, *, start_pos, final_norm):
    nb, seq, _ = x.shape
    tt = TILE_ROWS // nb
    assert seq % tt == 0 and tt >= POOL_HIST and tt % 8 == 0 and seq // tt >= 3
    assert D_FF // FF_CHUNK == 4, "the kernel body alternates four MLP chunks with four mixer pieces"
    n_tiles = seq // tt
    hist_rows = POOL_HIST * nb

    def resident(w):
        zeros = (0,) * (w.ndim - 1)
        return pl.BlockSpec((None,) + w.shape[1:], lambda i: (layer,) + zeros,
                            pipeline_mode=pl.Buffered(1))

    whole = pl.BlockSpec(memory_space=pltpu.VMEM)
    hbm = pl.BlockSpec(memory_space=pl.ANY)
    out_shape = (
        jax.ShapeDtypeStruct(x.shape, _F32),
        jax.ShapeDtypeStruct((nb, STATE_W), _F32),
        jax.ShapeDtypeStruct((nb, STATE_W), _F32),
        jax.ShapeDtypeStruct((hist_rows, WIDTH_B), _F32),
    )
    return pl.pallas_call(
        functools.partial(_layer_kernel, nb, tt, n_tiles, start_pos, final_norm),
        grid=(n_tiles + 1,),
        in_specs=[hbm, whole, whole, whole] + [resident(w) for w in weights],
        out_specs=(hbm, whole, whole, whole),
        out_shape=out_shape,
        scratch_shapes=[
            pltpu.VMEM((nb, STATE_W), _F32),
            pltpu.VMEM((nb, STATE_W), _F32),
            pltpu.VMEM((hist_rows + TILE_ROWS, WIDTH_B), _F32),
            pltpu.VMEM((TILE_ROWS // 2 + nb, 2 * STATE_W), _F32),
            pltpu.VMEM((TILE_ROWS, D_MODEL), _F32),
            pltpu.VMEM((TILE_ROWS, D_MODEL), _BF16),
            pltpu.VMEM((2, tt, nb, D_MODEL), _F32),
            pltpu.VMEM((2, tt, nb, D_MODEL), _F32),
            pltpu.SemaphoreType.DMA((2,)),
            pltpu.SemaphoreType.DMA((2,)),
        ],
        compiler_params=pltpu.CompilerParams(
            dimension_semantics=("arbitrary",), vmem_limit_bytes=VMEM_LIMIT_BYTES),
    )(x, h0_re, h0_im, hist0, *weights)


def _prep_weights(g_mix, g_ffn, g_final, w_in, lam_re, lam_im, log_dt, b_re, b_im, c_re, c_im,
                  d, w_glu, w_pool, pool_scale, w_out, w_ff1, w_ff2):
    depth = w_in.shape[0]
    dt = jnp.exp(log_dt)[..., None]
    mag = jnp.exp(lam_re * dt)
    a_re = mag * jnp.cos(lam_im * dt)
    a_im = mag * jnp.sin(lam_im * dt)
    den = lam_re * lam_re + lam_im * lam_im
    q_re = ((a_re - 1.0) * lam_re + a_im * lam_im) / den
    q_im = (a_im * lam_re - (a_re - 1.0) * lam_im) / den
    bb_re = q_re[..., None] * b_re - q_im[..., None] * b_im
    bb_im = q_re[..., None] * b_im + q_im[..., None] * b_re
    ab_re = a_re[..., None] * bb_re - a_im[..., None] * bb_im
    ab_im = a_re[..., None] * bb_im + a_im[..., None] * bb_re
    a2_re = a_re * a_re - a_im * a_im
    a2_im = 2.0 * a_re * a_im
    ca_re = c_re * a_re[:, :, None, :] - c_im * a_im[:, :, None, :]
    ca_im = c_re * a_im[:, :, None, :] + c_im * a_re[:, :, None, :]
    direct = (jnp.einsum('lgcp,lgpd->lgcd', c_re, bb_re)
              - jnp.einsum('lgcp,lgpd->lgcd', c_im, bb_im))

    def bu_tiles(bb):
        t = jnp.swapaxes(bb, -1, -2).reshape(depth, N_BU_HALF, BU_GROUPS, SSM_GROUP_CH, SSM_STATE)
        t = jnp.einsum('lnqcp,qr->lnqcrp', t, jnp.eye(BU_GROUPS, dtype=_F32))
        t = t.reshape(depth, N_BU_HALF, BU_GROUPS * SSM_GROUP_CH, BU_TILE)
        z = jnp.zeros_like(t)
        first = (jnp.arange(N_BU_HALF) % 2 == 0)[None, :, None, None]
        return jnp.where(first, jnp.concatenate([t, z], axis=2), jnp.concatenate([z, t], axis=2))

    def pair_tiles(ab, bb):
        return jnp.concatenate([bu_tiles(ab), bu_tiles(bb)], axis=2)

    def rd_block(cp):
        t = cp.reshape(depth, N_RD, RD_GROUPS, SSM_GROUP_CH, SSM_STATE)
        t = jnp.einsum('lmgcp,gh->lmgphc', t, jnp.eye(RD_GROUPS, dtype=_F32))
        return t.reshape(depth, N_RD, RD_K, RD_N)

    def rd_tiles(cp, cap):
        return jnp.concatenate([rd_block(cp), rd_block(cap)], axis=-1)

    n_dt = WIDTH_A // MXU_TILE
    g_dt = MXU_TILE // SSM_GROUP_CH
    wd = jnp.einsum('lmgcd,gh->lmgdhc', direct.reshape(depth, n_dt, g_dt, SSM_GROUP_CH, SSM_GROUP_CH),
                    jnp.eye(g_dt, dtype=_F32)).reshape(depth, n_dt, MXU_TILE, MXU_TILE)
    n_pp = len(POOL_WINDOWS) // 2
    wpool2 = jnp.einsum('lmkcd,kj->lmkcjd', w_pool.reshape(depth, n_pp, 2, POOL_GROUP_CH, POOL_GROUP_CH),
                        jnp.eye(2, dtype=_F32)).reshape(depth, n_pp, 2 * POOL_GROUP_CH, 2 * POOL_GROUP_CH)
    wp = jnp.concatenate([pair_tiles(ab_re, bb_re), pair_tiles(ab_im, bb_im)], axis=1)
    row = lambda v: v.reshape(depth, 1, -1).astype(_F32)
    g_fin = jnp.broadcast_to(g_final.reshape(1, 1, -1), (depth, 1, D_MODEL)).astype(_F32)
    return (row(g_mix), row(g_ffn), g_fin, w_in.astype(_BF16), wp.astype(_BF16),
            row(a2_re), row(a2_im), rd_tiles(c_re, ca_re).astype(_BF16),
            rd_tiles(-c_im, -ca_im).astype(_BF16), wd.astype(_BF16),
            row(d), w_glu.astype(_BF16), wpool2.astype(_BF16), row(pool_scale),
            w_out.astype(_BF16), w_ff1.astype(_BF16), w_ff2.astype(_BF16))


def _hist_to_time_major(hist):
    return jnp.swapaxes(hist, 0, 1).reshape(-1, hist.shape[-1])


def _hist_from_time_major(hist_tm, nb):
    return jnp.swapaxes(hist_tm.reshape(POOL_HIST, nb, -1), 0, 1)


def kernel(x_prompt, x_sample, state_ssm_re, state_ssm_im, state_pool, g_mix, g_ffn, w_in,
           ssm_lambda_re, ssm_lambda_im, ssm_log_dt, ssm_b_re, ssm_b_im, ssm_c_re, ssm_c_im,
           ssm_d, w_glu, w_pool, pool_scale, w_out, w_ff1, w_ff2, g_final):
    bp = x_prompt.shape[0]
    bs = x_sample.shape[0]
    weights = _prep_weights(g_mix, g_ffn, g_final, w_in, ssm_lambda_re, ssm_lambda_im, ssm_log_dt,
                            ssm_b_re, ssm_b_im, ssm_c_re, ssm_c_im, ssm_d, w_glu, w_pool,
                            pool_scale, w_out, w_ff1, w_ff2)
    xp, xs = x_prompt, x_sample
    zero_state = jnp.zeros((bp, STATE_W), _F32)
    zero_hist = jnp.zeros((POOL_HIST * bp, WIDTH_B), _F32)
    outs = {k: [] for k in ("p_re", "p_im", "p_pool", "s_re", "s_im", "s_pool")}
    for l in range(DEPTH):
        final_norm = l == DEPTH - 1
        xp, h_re, h_im, hist = _layer_call(xp, zero_state, zero_state, zero_hist, weights, l,
                                           start_pos=0, final_norm=final_norm)
        outs["p_re"].append(h_re.reshape(bp, SSM_GROUPS, SSM_STATE))
        outs["p_im"].append(h_im.reshape(bp, SSM_GROUPS, SSM_STATE))
        outs["p_pool"].append(_hist_from_time_major(hist, bp))
        xs, h_re, h_im, hist = _layer_call(
            xs, state_ssm_re[l].reshape(bs, STATE_W).astype(_F32),
            state_ssm_im[l].reshape(bs, STATE_W).astype(_F32),
            _hist_to_time_major(state_pool[l]).astype(_F32), weights, l,
            start_pos=PAST_LEN, final_norm=final_norm)
        outs["s_re"].append(h_re.reshape(bs, SSM_GROUPS, SSM_STATE))
        outs["s_im"].append(h_im.reshape(bs, SSM_GROUPS, SSM_STATE))
        outs["s_pool"].append(_hist_from_time_major(hist, bs))
    return (xp, xs,
            jnp.stack(outs["p_re"]), jnp.stack(outs["p_im"]), jnp.stack(outs["p_pool"]),
            jnp.stack(outs["s_re"]), jnp.stack(outs["s_im"]), jnp.stack(outs["s_pool"]))
```

```python
import functools

import jax
import jax.numpy as jnp
from jax import lax
from jax.experimental import pallas as pl
from jax.experimental.pallas import tpu as pltpu

D_MODEL = 1024
DEPTH = 2
PAST_LEN = 4096
WIDTH_A = 512
WIDTH_B = 512
MIX_WIDTH = WIDTH_A + WIDTH_B
SSM_GROUP_CH = 16
SSM_GROUPS = WIDTH_A // SSM_GROUP_CH
SSM_STATE = 64
STATE_W = SSM_GROUPS * SSM_STATE
POOL_WINDOWS = (2, 4, 8, 16)
POOL_GROUP_CH = WIDTH_B // len(POOL_WINDOWS)
POOL_HIST = max(POOL_WINDOWS) - 1
D_FF = 4 * D_MODEL
RMS_EPS = 1e-5

TILE_ROWS = 512
FF_CHUNK = 1024
OUT_CHUNK = 256
MXU_TILE = 256
BU_TILE = MXU_TILE
BU_GROUPS = BU_TILE // SSM_STATE
BU_K = 128
N_BU_HALF = STATE_W // BU_TILE
RD_GROUPS = 8
RD_K = RD_GROUPS * SSM_STATE
RD_N = RD_GROUPS * SSM_GROUP_CH
N_RD = SSM_GROUPS // RD_GROUPS
SCAN_LANES = 128
VMEM_LIMIT_BYTES = 60 * 1024 * 1024

_BF16 = jnp.bfloat16
_F32 = jnp.float32


def _rms_scale(x):
    ms = jnp.mean(x * x, axis=-1, keepdims=True)
    return x * lax.rsqrt(ms + RMS_EPS)


def _bu_k_block(n):
    return (n % N_BU_HALF) * BU_GROUPS * SSM_GROUP_CH // BU_K


def _layer_kernel(nb, tt, n_tiles, start_pos, final_norm,
                  x_hbm, h0re_ref, h0im_ref, hist0_ref,
                  gfin_ref, win_ref, wp_ref, a2re_ref, a2im_ref,
                  wrre_ref, wrim_ref, wd_ref, d_ref, wglu_ref, wpool_ref, pscale_ref,
                  wout_ref, wff1_ref, wff2_ref,
                  y_hbm, hre_out, him_out, hist_out,
                  hre, him, ext, hs, x1_buf, hf_buf, mid_buf, xbuf, ybuf, sem_in, sem_out):
    rows = nb * tt
    half = rows // 2
    hist_rows = POOL_HIST * nb
    i = pl.program_id(0)
    live = i < n_tiles

    slot = i % 2
    yslot = 1 - slot

    def x_copies(tile, s):
        t0 = pl.multiple_of(tile * tt, tt)
        return [pltpu.make_async_copy(x_hbm.at[b, pl.ds(t0, tt), :], xbuf.at[s, :, b, :], sem_in.at[s])
                for b in range(nb)]

    def y_copies(tile, s):
        t0 = pl.multiple_of(tile * tt, tt)
        return [pltpu.make_async_copy(ybuf.at[s, :, b, :], y_hbm.at[b, pl.ds(t0, tt), :], sem_out.at[s])
                for b in range(nb)]

    @pl.when(i == 0)
    def _():
        for c in x_copies(0, 0):
            c.start()
        hre[...] = h0re_ref[...]
        him[...] = h0im_ref[...]
        ext[0:hist_rows, :] = hist0_ref[...]
        x1_buf[...] = jnp.zeros((rows, D_MODEL), _F32)
        hf_buf[...] = jnp.zeros((rows, D_MODEL), _BF16)

    @pl.when(i + 1 < n_tiles)
    def _():
        for c in x_copies(i + 1, yslot):
            c.start()

    @pl.when(i < n_tiles)
    def _():
        for c in x_copies(i, slot):
            c.wait()

    @pl.when(i >= 3)
    def _():
        for c in y_copies(i - 3, yslot):
            c.wait()

    x1_prev = x1_buf[...]
    hf = hf_buf[...]

    def mlp_up(c):
        cs = slice(c * FF_CHUNK, (c + 1) * FF_CHUNK)
        mid = jnp.dot(hf, wff1_ref[:, cs], preferred_element_type=_F32)
        mid_buf[:, cs] = jnp.square(jnp.maximum(mid, 0.0)).astype(_BF16)

    def mlp_down(n):
        ns = slice(n * OUT_CHUNK, (n + 1) * OUT_CHUNK)
        return x1_prev[:, ns] + jnp.dot(mid_buf[...], wff2_ref[:, ns], preferred_element_type=_F32)

    mlp_up(0)
    mlp_up(1)

    x = xbuf[slot].reshape(rows, D_MODEL)
    hn = _rms_scale(x).astype(_BF16)
    proj = jnp.dot(hn, win_ref[...], preferred_element_type=_F32)
    ua = proj[:, :WIDTH_A]
    ub = proj[:, WIDTH_A:]
    ua_pairs = ua.reshape(tt // 2, 2, nb, WIDTH_A)
    ue_bf = ua_pairs[:, 0].reshape(half, WIDTH_A).astype(_BF16)
    uo_bf = ua_pairs[:, 1].reshape(half, WIDTH_A).astype(_BF16)

    for n in range(2 * N_BU_HALF):
        ks = slice(_bu_k_block(n) * BU_K, (_bu_k_block(n) + 1) * BU_K)
        lhs = jnp.concatenate([ue_bf[:, ks], uo_bf[:, ks]], axis=1)
        hs[nb:nb + half, n * BU_TILE:(n + 1) * BU_TILE] = jnp.dot(
            lhs, wp_ref[n], preferred_element_type=_F32)

    lc = SCAN_LANES
    n_lane_chunks = STATE_W // lc
    always = i >= 0

    def scan_lane_chunk(c, chain):
        re_sl = slice(c * lc, (c + 1) * lc)
        im_sl = slice(STATE_W + c * lc, STATE_W + (c + 1) * lc)
        a_re = jnp.broadcast_to(a2re_ref[:, re_sl], (nb, lc))
        a_im = jnp.broadcast_to(a2im_ref[:, re_sl], (nb, lc))
        h_r0 = hre[:, re_sl]
        h_i0 = him[:, re_sl]
        hs[0:nb, re_sl] = h_r0
        hs[0:nb, im_sl] = h_i0
        h_r, h_i = h_r0, h_i0
        if chain is not None:
            h_r = jnp.where(always, h_r, chain)
        for k in range(tt // 2):
            rs = slice((k + 1) * nb, (k + 2) * nb)
            n_r = a_re * h_r - a_im * h_i + hs[rs, re_sl]
            n_i = a_re * h_i + a_im * h_r + hs[rs, im_sl]
            hs[rs, re_sl] = n_r
            hs[rs, im_sl] = n_i
            h_r, h_i = n_r, n_i
        hre[:, re_sl] = jnp.where(live, h_r, h_r0)
        him[:, re_sl] = jnp.where(live, h_i, h_i0)
        return h_r

    chain = None
    for c in range(n_lane_chunks // 2):
        chain = scan_lane_chunk(c, chain)

    mlp_up(2)
    mlp_up(3)

    for c in range(n_lane_chunks // 2, n_lane_chunks):
        chain = scan_lane_chunk(c, chain)
    hre_out[...] = hre[...]
    him_out[...] = him[...]
    y_odd, y_even = [], []
    for m in range(N_RD):
        h_re = hs[:, m * RD_K:(m + 1) * RD_K].astype(_BF16)
        h_im = hs[:, STATE_W + m * RD_K:STATE_W + (m + 1) * RD_K].astype(_BF16)
        out = (jnp.dot(h_re, wrre_ref[m], preferred_element_type=_F32)
               + jnp.dot(h_im, wrim_ref[m], preferred_element_type=_F32))
        y_odd.append(out[nb:, :RD_N])
        y_even.append(out[:half, RD_N:])
    direct = [jnp.dot(ue_bf[:, m * MXU_TILE:(m + 1) * MXU_TILE], wd_ref[m],
                      preferred_element_type=_F32) for m in range(WIDTH_A // MXU_TILE)]
    y_even = jnp.concatenate(y_even, axis=1) + jnp.concatenate(direct, axis=1)
    y_odd = jnp.concatenate(y_odd, axis=1)
    ys = jnp.stack([y_even.reshape(tt // 2, nb, WIDTH_A), y_odd.reshape(tt // 2, nb, WIDTH_A)],
                   axis=1).reshape(rows, WIDTH_A)
    y = ys + d_ref[...] * ua
    z = jax.nn.gelu(y)
    out_a = z * jax.nn.sigmoid(jnp.dot(z.astype(_BF16), wglu_ref[...], preferred_element_type=_F32))

    y_cols = [mlp_down(0), mlp_down(1)]

    ext[hist_rows:hist_rows + rows, :] = ub
    t0 = start_pos + jnp.minimum(i, n_tiles - 1) * tt
    pos = t0 + lax.broadcasted_iota(jnp.int32, (rows, POOL_GROUP_CH), 0) // nb
    n_seen = (pos + 1).astype(_F32)
    diffs = []
    for k, w in enumerate(POOL_WINDOWS):
        ls = slice(k * POOL_GROUP_CH, (k + 1) * POOL_GROUP_CH)
        wsum = ext[(POOL_HIST - (w - 1)) * nb:hist_rows + rows, ls]
        shift = 1
        while shift < w:
            wsum = wsum[shift * nb:] + wsum[:-shift * nb]
            shift *= 2
        cnt = jnp.minimum(n_seen, float(w))
        diffs.append((wsum / cnt - ub[:, ls]).astype(_BF16))
    yb_parts = [jnp.dot(jnp.concatenate(diffs[2 * m:2 * m + 2], axis=1), wpool_ref[m],
                        preferred_element_type=_F32) for m in range(len(POOL_WINDOWS) // 2)]
    yb = jnp.concatenate(yb_parts, axis=1) * pscale_ref[...]
    new_hist = jnp.where(live, ext[rows:rows + hist_rows, :], ext[0:hist_rows, :])
    hist_out[...] = new_hist
    ext[0:hist_rows, :] = new_hist

    mixed = jnp.concatenate([out_a, yb], axis=1)
    x1 = x + jnp.dot(mixed.astype(_BF16), wout_ref[...], preferred_element_type=_F32)

    y_cols += [mlp_down(2), mlp_down(3)]
    acc = jnp.concatenate(y_cols, axis=1)
    if final_norm:
        acc = _rms_scale(acc) * gfin_ref[...]
    ybuf[yslot] = acc.reshape(tt, nb, D_MODEL)
    x1_buf[...] = x1
    hf_buf[...] = _rms_scale(x1).astype(_BF16)

    @pl.when(i >= 1)
    def _():
        for c in y_copies(i - 1, yslot):
            c.start()

    @pl.when(i == n_tiles)
    def _():
        for c in y_copies(i - 2, slot) + y_copies(i - 1, yslot):
            c.wait()


def _layer_call(x, h0_re, h0_im, hist0, weights, layer, *, start_pos, final_norm):
    nb, seq, _ = x.shape
    tt = TILE_ROWS // nb
    assert seq % tt == 0 and tt >= POOL_HIST and tt % 8 == 0 and seq // tt >= 3
    assert D_FF // FF_CHUNK == 4 and D_MODEL // OUT_CHUNK == 4, "the kernel body is written for 4 + 4 MLP chunks"
    n_tiles = seq // tt
    hist_rows = POOL_HIST * nb

    def resident(w):
        zeros = (0,) * (w.ndim - 1)
        return pl.BlockSpec((None,) + w.shape[1:], lambda i: (layer,) + zeros,
                            pipeline_mode=pl.Buffered(1))

    whole = pl.BlockSpec(memory_space=pltpu.VMEM)
    hbm = pl.BlockSpec(memory_space=pl.ANY)
    out_shape = (
        jax.ShapeDtypeStruct(x.shape, _F32),
        jax.ShapeDtypeStruct((nb, STATE_W), _F32),
        jax.ShapeDtypeStruct((nb, STATE_W), _F32),
        jax.ShapeDtypeStruct((hist_rows, WIDTH_B), _F32),
    )
    return pl.pallas_call(
        functools.partial(_layer_kernel, nb, tt, n_tiles, start_pos, final_norm),
        grid=(n_tiles + 1,),
        in_specs=[hbm, whole, whole, whole] + [resident(w) for w in weights],
        out_specs=(hbm, whole, whole, whole),
        out_shape=out_shape,
        scratch_shapes=[
            pltpu.VMEM((nb, STATE_W), _F32),
            pltpu.VMEM((nb, STATE_W), _F32),
            pltpu.VMEM((hist_rows + TILE_ROWS, WIDTH_B), _F32),
            pltpu.VMEM((TILE_ROWS // 2 + nb, 2 * STATE_W), _F32),
            pltpu.VMEM((TILE_ROWS, D_MODEL), _F32),
            pltpu.VMEM((TILE_ROWS, D_MODEL), _BF16),
            pltpu.VMEM((TILE_ROWS, D_FF), _BF16),
            pltpu.VMEM((2, tt, nb, D_MODEL), _F32),
            pltpu.VMEM((2, tt, nb, D_MODEL), _F32),
            pltpu.SemaphoreType.DMA((2,)),
            pltpu.SemaphoreType.DMA((2,)),
        ],
        compiler_params=pltpu.CompilerParams(
            dimension_semantics=("arbitrary",), vmem_limit_bytes=VMEM_LIMIT_BYTES),
    )(x, h0_re, h0_im, hist0, *weights)


def _prep_weights(g_mix, g_ffn, g_final, w_in, lam_re, lam_im, log_dt, b_re, b_im, c_re, c_im,
                  d, w_glu, w_pool, pool_scale, w_out, w_ff1, w_ff2):
    depth = w_in.shape[0]
    dt = jnp.exp(log_dt)[..., None]
    mag = jnp.exp(lam_re * dt)
    a_re = mag * jnp.cos(lam_im * dt)
    a_im = mag * jnp.sin(lam_im * dt)
    den = lam_re * lam_re + lam_im * lam_im
    q_re = ((a_re - 1.0) * lam_re + a_im * lam_im) / den
    q_im = (a_im * lam_re - (a_re - 1.0) * lam_im) / den
    bb_re = q_re[..., None] * b_re - q_im[..., None] * b_im
    bb_im = q_re[..., None] * b_im + q_im[..., None] * b_re
    ab_re = a_re[..., None] * bb_re - a_im[..., None] * bb_im
    ab_im = a_re[..., None] * bb_im + a_im[..., None] * bb_re
    a2_re = a_re * a_re - a_im * a_im
    a2_im = 2.0 * a_re * a_im
    ca_re = c_re * a_re[:, :, None, :] - c_im * a_im[:, :, None, :]
    ca_im = c_re * a_im[:, :, None, :] + c_im * a_re[:, :, None, :]
    direct = (jnp.einsum('lgcp,lgpd->lgcd', c_re, bb_re)
              - jnp.einsum('lgcp,lgpd->lgcd', c_im, bb_im))

    def bu_tiles(bb):
        t = jnp.swapaxes(bb, -1, -2).reshape(depth, N_BU_HALF, BU_GROUPS, SSM_GROUP_CH, SSM_STATE)
        t = jnp.einsum('lnqcp,qr->lnqcrp', t, jnp.eye(BU_GROUPS, dtype=_F32))
        t = t.reshape(depth, N_BU_HALF, BU_GROUPS * SSM_GROUP_CH, BU_TILE)
        z = jnp.zeros_like(t)
        first = (jnp.arange(N_BU_HALF) % 2 == 0)[None, :, None, None]
        return jnp.where(first, jnp.concatenate([t, z], axis=2), jnp.concatenate([z, t], axis=2))

    def pair_tiles(ab, bb):
        return jnp.concatenate([bu_tiles(ab), bu_tiles(bb)], axis=2)

    def rd_block(cp):
        t = cp.reshape(depth, N_RD, RD_GROUPS, SSM_GROUP_CH, SSM_STATE)
        t = jnp.einsum('lmgcp,gh->lmgphc', t, jnp.eye(RD_GROUPS, dtype=_F32))
        return t.reshape(depth, N_RD, RD_K, RD_N)

    def rd_tiles(cp, cap):
        return jnp.concatenate([rd_block(cp), rd_block(cap)], axis=-1)

    n_dt = WIDTH_A // MXU_TILE
    g_dt = MXU_TILE // SSM_GROUP_CH
    wd = jnp.einsum('lmgcd,gh->lmgdhc', direct.reshape(depth, n_dt, g_dt, SSM_GROUP_CH, SSM_GROUP_CH),
                    jnp.eye(g_dt, dtype=_F32)).reshape(depth, n_dt, MXU_TILE, MXU_TILE)
    n_pp = len(POOL_WINDOWS) // 2
    wpool2 = jnp.einsum('lmkcd,kj->lmkcjd', w_pool.reshape(depth, n_pp, 2, POOL_GROUP_CH, POOL_GROUP_CH),
                        jnp.eye(2, dtype=_F32)).reshape(depth, n_pp, 2 * POOL_GROUP_CH, 2 * POOL_GROUP_CH)
    wp = jnp.concatenate([pair_tiles(ab_re, bb_re), pair_tiles(ab_im, bb_im)], axis=1)
    row = lambda v: v.reshape(depth, 1, -1).astype(_F32)
    g_fin = jnp.broadcast_to(g_final.reshape(1, 1, -1), (depth, 1, D_MODEL)).astype(_F32)
    w_in_g = g_mix.astype(_F32)[:, :, None] * w_in
    w_ff1_g = g_ffn.astype(_F32)[:, :, None] * w_ff1
    return (g_fin, w_in_g.astype(_BF16), wp.astype(_BF16),
            row(a2_re), row(a2_im), rd_tiles(c_re, ca_re).astype(_BF16),
            rd_tiles(-c_im, -ca_im).astype(_BF16), wd.astype(_BF16),
            row(d), w_glu.astype(_BF16), wpool2.astype(_BF16), row(pool_scale),
            w_out.astype(_BF16), w_ff1_g.astype(_BF16), w_ff2.astype(_BF16))


def _hist_to_time_major(hist):
    return jnp.swapaxes(hist, 0, 1).reshape(-1, hist.shape[-1])


def _hist_from_time_major(hist_tm, nb):
    return jnp.swapaxes(hist_tm.reshape(POOL_HIST, nb, -1), 0, 1)


def kernel(x_prompt, x_sample, state_ssm_re, state_ssm_im, state_pool, g_mix, g_ffn, w_in,
           ssm_lambda_re, ssm_lambda_im, ssm_log_dt, ssm_b_re, ssm_b_im, ssm_c_re, ssm_c_im,
           ssm_d, w_glu, w_pool, pool_scale, w_out, w_ff1, w_ff2, g_final):
    bp = x_prompt.shape[0]
    bs = x_sample.shape[0]
    weights = _prep_weights(g_mix, g_ffn, g_final, w_in, ssm_lambda_re, ssm_lambda_im, ssm_log_dt,
                            ssm_b_re, ssm_b_im, ssm_c_re, ssm_c_im, ssm_d, w_glu, w_pool,
                            pool_scale, w_out, w_ff1, w_ff2)
    xp, xs = x_prompt, x_sample
    zero_state = jnp.zeros((bp, STATE_W), _F32)
    zero_hist = jnp.zeros((POOL_HIST * bp, WIDTH_B), _F32)
    outs = {k: [] for k in ("p_re", "p_im", "p_pool", "s_re", "s_im", "s_pool")}
    for l in range(DEPTH):
        final_norm = l == DEPTH - 1
        xp, h_re, h_im, hist = _layer_call(xp, zero_state, zero_state, zero_hist, weights, l,
                                           start_pos=0, final_norm=final_norm)
        outs["p_re"].append(h_re.reshape(bp, SSM_GROUPS, SSM_STATE))
        outs["p_im"].append(h_im.reshape(bp, SSM_GROUPS, SSM_STATE))
        outs["p_pool"].append(_hist_from_time_major(hist, bp))
        xs, h_re, h_im, hist = _layer_call(
            xs, state_ssm_re[l].reshape(bs, STATE_W).astype(_F32),
            state_ssm_im[l].reshape(bs, STATE_W).astype(_F32),
            _hist_to_time_major(state_pool[l]).astype(_F32), weights, l,
            start_pos=PAST_LEN, final_norm=final_norm)
        outs["s_re"].append(h_re.reshape(bs, SSM_GROUPS, SSM_STATE))
        outs["s_im"].append(h_im.reshape(bs, SSM_GROUPS, SSM_STATE))
        outs["s_pool"].append(_hist_from_time_major(hist, bs))
    return (xp, xs,
            jnp.stack(outs["p_re"]), jnp.stack(outs["p_im"]), jnp.stack(outs["p_pool"]),
            jnp.stack(outs["s_re"]), jnp.stack(outs["s_im"]), jnp.stack(outs["s_pool"]))
```

```python
import functools

import jax
import jax.numpy as jnp
from jax import lax
from jax.experimental import pallas as pl
from jax.experimental.pallas import tpu as pltpu

D_MODEL = 1024
DEPTH = 2
PAST_LEN = 4096
WIDTH_A = 512
WIDTH_B = 512
MIX_WIDTH = WIDTH_A + WIDTH_B
SSM_GROUP_CH = 16
SSM_GROUPS = WIDTH_A // SSM_GROUP_CH
SSM_STATE = 64
STATE_W = SSM_GROUPS * SSM_STATE
POOL_WINDOWS = (2, 4, 8, 16)
POOL_GROUP_CH = WIDTH_B // len(POOL_WINDOWS)
POOL_HIST = max(POOL_WINDOWS) - 1
D_FF = 4 * D_MODEL
RMS_EPS = 1e-5

TILE_ROWS = 512
FF_CHUNK = 1024
OUT_CHUNK = 256
MXU_TILE = 256
BU_TILE = MXU_TILE
BU_GROUPS = BU_TILE // SSM_STATE
BU_K = 128
N_BU_HALF = STATE_W // BU_TILE
RD_GROUPS = 8
RD_K = RD_GROUPS * SSM_STATE
RD_N = RD_GROUPS * SSM_GROUP_CH
N_RD = SSM_GROUPS // RD_GROUPS
SCAN_LANES = 128
VMEM_LIMIT_BYTES = 60 * 1024 * 1024

_BF16 = jnp.bfloat16
_F32 = jnp.float32


def _rms_scale(x):
    ms = jnp.mean(x * x, axis=-1, keepdims=True)
    return x * lax.rsqrt(ms + RMS_EPS)


def _bu_k_block(n):
    return (n % N_BU_HALF) * BU_GROUPS * SSM_GROUP_CH // BU_K


def _layer_kernel(nb, tt, n_p, per_group, n_tiles, final_norm,
                  xp_hbm, xs_hbm, h0re_ref, h0im_ref, hist0_ref,
                  gfin_ref, win_ref, wp_ref, a2re_ref, a2im_ref,
                  wrre_ref, wrim_ref, wd_ref, d_ref, wglu_ref, wpool_ref, pscale_ref,
                  wout_ref, wff1_ref, wff2_ref,
                  yp_hbm, ys_hbm, hre_out, him_out, hist_out,
                  hre, him, ext, hs, x1_buf, hf_buf, mid_buf, xbuf, ybuf, sem_in, sem_out):
    rows = nb * tt
    half = rows // 2
    hist_rows = POOL_HIST * nb
    i = pl.program_id(0)
    live = i < n_tiles

    slot = i % 2
    yslot = 1 - slot
    tile_m = jnp.minimum(i, n_tiles - 1)
    samp_m = jnp.maximum(tile_m - n_p, 0)
    is_prompt_m = tile_m < n_p
    seg = jnp.where(is_prompt_m, 0, 1 + samp_m // per_group)
    first_of_seg = jnp.where(is_prompt_m, tile_m == 0, samp_m % per_group == 0)

    def tile_copies(tile, s, buf, sem, p_hbm, s_hbm, to_vmem, op):
        def run(hbm, row0, t0):
            for b in range(nb):
                win = hbm.at[row0 + b, pl.ds(pl.multiple_of(t0, tt), tt), :]
                vm = buf.at[s, :, b, :]
                c = pltpu.make_async_copy(win, vm, sem.at[s]) if to_vmem else \
                    pltpu.make_async_copy(vm, win, sem.at[s])
                getattr(c, op)()

        @pl.when(tile < n_p)
        def _():
            run(p_hbm, 0, tile * tt)

        @pl.when(tile >= n_p)
        def _():
            j = tile - n_p
            run(s_hbm, (j // per_group) * nb, (j % per_group) * tt)

    def x_tile(tile, s, op):
        tile_copies(tile, s, xbuf, sem_in, xp_hbm, xs_hbm, True, op)

    def y_tile(tile, s, op):
        tile_copies(tile, s, ybuf, sem_out, yp_hbm, ys_hbm, False, op)

    @pl.when(i == 0)
    def _():
        x_tile(i, 0, "start")
        x1_buf[...] = jnp.zeros((rows, D_MODEL), _F32)
        hf_buf[...] = jnp.zeros((rows, D_MODEL), _BF16)

    @pl.when(first_of_seg & live)
    def _():
        hre[...] = h0re_ref[seg]
        him[...] = h0im_ref[seg]
        ext[0:hist_rows, :] = hist0_ref[seg]

    @pl.when(i + 1 < n_tiles)
    def _():
        x_tile(i + 1, yslot, "start")

    @pl.when(i < n_tiles)
    def _():
        x_tile(i, slot, "wait")

    @pl.when(i >= 3)
    def _():
        y_tile(i - 3, yslot, "wait")

    x1_prev = x1_buf[...]
    hf = hf_buf[...]

    def mlp_up(c):
        cs = slice(c * FF_CHUNK, (c + 1) * FF_CHUNK)
        mid = jnp.dot(hf, wff1_ref[:, cs], preferred_element_type=_F32)
        mid_buf[:, cs] = jnp.square(jnp.maximum(mid, 0.0)).astype(_BF16)

    def mlp_down(n):
        ns = slice(n * OUT_CHUNK, (n + 1) * OUT_CHUNK)
        return x1_prev[:, ns] + jnp.dot(mid_buf[...], wff2_ref[:, ns], preferred_element_type=_F32)

    mlp_up(0)
    mlp_up(1)

    x = xbuf[slot].reshape(rows, D_MODEL)
    hn = _rms_scale(x).astype(_BF16)
    proj = jnp.dot(hn, win_ref[...], preferred_element_type=_F32)
    ua = proj[:, :WIDTH_A]
    ub = proj[:, WIDTH_A:]
    ua_pairs = ua.reshape(tt // 2, 2, nb, WIDTH_A)
    ue_bf = ua_pairs[:, 0].reshape(half, WIDTH_A).astype(_BF16)
    uo_bf = ua_pairs[:, 1].reshape(half, WIDTH_A).astype(_BF16)

    for n in range(2 * N_BU_HALF):
        ks = slice(_bu_k_block(n) * BU_K, (_bu_k_block(n) + 1) * BU_K)
        lhs = jnp.concatenate([ue_bf[:, ks], uo_bf[:, ks]], axis=1)
        hs[nb:nb + half, n * BU_TILE:(n + 1) * BU_TILE] = jnp.dot(
            lhs, wp_ref[n], preferred_element_type=_F32)

    lc = SCAN_LANES
    n_lane_chunks = STATE_W // lc
    always = i >= 0

    def scan_lane_chunk(c, chain):
        re_sl = slice(c * lc, (c + 1) * lc)
        im_sl = slice(STATE_W + c * lc, STATE_W + (c + 1) * lc)
        a_re = jnp.broadcast_to(a2re_ref[:, re_sl], (nb, lc))
        a_im = jnp.broadcast_to(a2im_ref[:, re_sl], (nb, lc))
        h_r0 = hre[:, re_sl]
        h_i0 = him[:, re_sl]
        hs[0:nb, re_sl] = h_r0
        hs[0:nb, im_sl] = h_i0
        h_r, h_i = h_r0, h_i0
        if chain is not None:
            h_r = jnp.where(always, h_r, chain)
        for k in range(tt // 2):
            rs = slice((k + 1) * nb, (k + 2) * nb)
            n_r = a_re * h_r - a_im * h_i + hs[rs, re_sl]
            n_i = a_re * h_i + a_im * h_r + hs[rs, im_sl]
            hs[rs, re_sl] = n_r
            hs[rs, im_sl] = n_i
            h_r, h_i = n_r, n_i
        hre[:, re_sl] = jnp.where(live, h_r, h_r0)
        him[:, re_sl] = jnp.where(live, h_i, h_i0)
        return h_r

    chain = None
    for c in range(n_lane_chunks // 2):
        chain = scan_lane_chunk(c, chain)

    mlp_up(2)
    mlp_up(3)

    for c in range(n_lane_chunks // 2, n_lane_chunks):
        chain = scan_lane_chunk(c, chain)
    hre_out[seg] = hre[...]
    him_out[seg] = him[...]
    y_odd, y_even = [], []
    for m in range(N_RD):
        h_re = hs[:, m * RD_K:(m + 1) * RD_K].astype(_BF16)
        h_im = hs[:, STATE_W + m * RD_K:STATE_W + (m + 1) * RD_K].astype(_BF16)
        out = (jnp.dot(h_re, wrre_ref[m], preferred_element_type=_F32)
               + jnp.dot(h_im, wrim_ref[m], preferred_element_type=_F32))
        y_odd.append(out[nb:, :RD_N])
        y_even.append(out[:half, RD_N:])
    direct = [jnp.dot(ue_bf[:, m * MXU_TILE:(m + 1) * MXU_TILE], wd_ref[m],
                      preferred_element_type=_F32) for m in range(WIDTH_A // MXU_TILE)]
    y_even = jnp.concatenate(y_even, axis=1) + jnp.concatenate(direct, axis=1)
    y_odd = jnp.concatenate(y_odd, axis=1)
    ys = jnp.stack([y_even.reshape(tt // 2, nb, WIDTH_A), y_odd.reshape(tt // 2, nb, WIDTH_A)],
                   axis=1).reshape(rows, WIDTH_A)
    y = ys + d_ref[...] * ua
    z = jax.nn.gelu(y)
    out_a = z * jax.nn.sigmoid(jnp.dot(z.astype(_BF16), wglu_ref[...], preferred_element_type=_F32))

    y_cols = [mlp_down(0), mlp_down(1)]

    ext[hist_rows:hist_rows + rows, :] = ub
    t0 = jnp.where(is_prompt_m, tile_m * tt, PAST_LEN + (samp_m % per_group) * tt)
    pos = t0 + lax.broadcasted_iota(jnp.int32, (rows, POOL_GROUP_CH), 0) // nb
    n_seen = (pos + 1).astype(_F32)
    diffs = []
    for k, w in enumerate(POOL_WINDOWS):
        ls = slice(k * POOL_GROUP_CH, (k + 1) * POOL_GROUP_CH)
        wsum = ext[(POOL_HIST - (w - 1)) * nb:hist_rows + rows, ls]
        shift = 1
        while shift < w:
            wsum = wsum[shift * nb:] + wsum[:-shift * nb]
            shift *= 2
        cnt = jnp.minimum(n_seen, float(w))
        diffs.append((wsum / cnt - ub[:, ls]).astype(_BF16))
    yb_parts = [jnp.dot(jnp.concatenate(diffs[2 * m:2 * m + 2], axis=1), wpool_ref[m],
                        preferred_element_type=_F32) for m in range(len(POOL_WINDOWS) // 2)]
    yb = jnp.concatenate(yb_parts, axis=1) * pscale_ref[...]
    new_hist = jnp.where(live, ext[rows:rows + hist_rows, :], ext[0:hist_rows, :])
    hist_out[seg] = new_hist
    ext[0:hist_rows, :] = new_hist

    mixed = jnp.concatenate([out_a, yb], axis=1)
    x1 = x + jnp.dot(mixed.astype(_BF16), wout_ref[...], preferred_element_type=_F32)

    y_cols += [mlp_down(2), mlp_down(3)]
    acc = jnp.concatenate(y_cols, axis=1)
    if final_norm:
        acc = _rms_scale(acc) * gfin_ref[...]
    ybuf[yslot] = acc.reshape(tt, nb, D_MODEL)
    x1_buf[...] = x1
    hf_buf[...] = _rms_scale(x1).astype(_BF16)

    @pl.when(i >= 1)
    def _():
        y_tile(i - 1, yslot, "start")

    @pl.when(i == n_tiles)
    def _():
        y_tile(i - 2, slot, "wait")
        y_tile(i - 1, yslot, "wait")


def _layer_call(xp, xs, h0_re, h0_im, hist0, weights, layer, *, final_norm):
    nb, seq_p, _ = xp.shape
    bs, seq_s, _ = xs.shape
    tt = TILE_ROWS // nb
    assert seq_p % tt == 0 and seq_s % tt == 0 and bs % nb == 0 and tt >= POOL_HIST and tt % 8 == 0
    assert D_FF // FF_CHUNK == 4 and D_MODEL // OUT_CHUNK == 4, "the kernel body is written for 4 + 4 MLP chunks"
    n_p = seq_p // tt
    per_group = seq_s // tt
    n_seg = 1 + bs // nb
    n_tiles = n_p + (bs // nb) * per_group
    assert n_tiles >= 3
    hist_rows = POOL_HIST * nb

    def resident(w):
        zeros = (0,) * (w.ndim - 1)
        return pl.BlockSpec((None,) + w.shape[1:], lambda i: (layer,) + zeros,
                            pipeline_mode=pl.Buffered(1))

    whole = pl.BlockSpec(memory_space=pltpu.VMEM)
    hbm = pl.BlockSpec(memory_space=pl.ANY)
    out_shape = (
        jax.ShapeDtypeStruct(xp.shape, _F32),
        jax.ShapeDtypeStruct(xs.shape, _F32),
        jax.ShapeDtypeStruct((n_seg, nb, STATE_W), _F32),
        jax.ShapeDtypeStruct((n_seg, nb, STATE_W), _F32),
        jax.ShapeDtypeStruct((n_seg, hist_rows, WIDTH_B), _F32),
    )
    return pl.pallas_call(
        functools.partial(_layer_kernel, nb, tt, n_p, per_group, n_tiles, final_norm),
        grid=(n_tiles + 1,),
        in_specs=[hbm, hbm, whole, whole, whole] + [resident(w) for w in weights],
        out_specs=(hbm, hbm, whole, whole, whole),
        out_shape=out_shape,
        scratch_shapes=[
            pltpu.VMEM((nb, STATE_W), _F32),
            pltpu.VMEM((nb, STATE_W), _F32),
            pltpu.VMEM((hist_rows + TILE_ROWS, WIDTH_B), _F32),
            pltpu.VMEM((TILE_ROWS // 2 + nb, 2 * STATE_W), _F32),
            pltpu.VMEM((TILE_ROWS, D_MODEL), _F32),
            pltpu.VMEM((TILE_ROWS, D_MODEL), _BF16),
            pltpu.VMEM((TILE_ROWS, D_FF), _BF16),
            pltpu.VMEM((2, tt, nb, D_MODEL), _F32),
            pltpu.VMEM((2, tt, nb, D_MODEL), _F32),
            pltpu.SemaphoreType.DMA((2,)),
            pltpu.SemaphoreType.DMA((2,)),
        ],
        compiler_params=pltpu.CompilerParams(
            dimension_semantics=("arbitrary",), vmem_limit_bytes=VMEM_LIMIT_BYTES),
    )(xp, xs, h0_re, h0_im, hist0, *weights)


def _prep_weights(g_mix, g_ffn, g_final, w_in, lam_re, lam_im, log_dt, b_re, b_im, c_re, c_im,
                  d, w_glu, w_pool, pool_scale, w_out, w_ff1, w_ff2):
    depth = w_in.shape[0]
    dt = jnp.exp(log_dt)[..., None]
    mag = jnp.exp(lam_re * dt)
    a_re = mag * jnp.cos(lam_im * dt)
    a_im = mag * jnp.sin(lam_im * dt)
    den = lam_re * lam_re + lam_im * lam_im
    q_re = ((a_re - 1.0) * lam_re + a_im * lam_im) / den
    q_im = (a_im * lam_re - (a_re - 1.0) * lam_im) / den
    bb_re = q_re[..., None] * b_re - q_im[..., None] * b_im
    bb_im = q_re[..., None] * b_im + q_im[..., None] * b_re
    ab_re = a_re[..., None] * bb_re - a_im[..., None] * bb_im
    ab_im = a_re[..., None] * bb_im + a_im[..., None] * bb_re
    a2_re = a_re * a_re - a_im * a_im
    a2_im = 2.0 * a_re * a_im
    ca_re = c_re * a_re[:, :, None, :] - c_im * a_im[:, :, None, :]
    ca_im = c_re * a_im[:, :, None, :] + c_im * a_re[:, :, None, :]
    direct = (jnp.einsum('lgcp,lgpd->lgcd', c_re, bb_re)
              - jnp.einsum('lgcp,lgpd->lgcd', c_im, bb_im))

    def bu_tiles(bb):
        t = jnp.swapaxes(bb, -1, -2).reshape(depth, N_BU_HALF, BU_GROUPS, SSM_GROUP_CH, SSM_STATE)
        t = jnp.einsum('lnqcp,qr->lnqcrp', t, jnp.eye(BU_GROUPS, dtype=_F32))
        t = t.reshape(depth, N_BU_HALF, BU_GROUPS * SSM_GROUP_CH, BU_TILE)
        z = jnp.zeros_like(t)
        first = (jnp.arange(N_BU_HALF) % 2 == 0)[None, :, None, None]
        return jnp.where(first, jnp.concatenate([t, z], axis=2), jnp.concatenate([z, t], axis=2))

    def pair_tiles(ab, bb):
        return jnp.concatenate([bu_tiles(ab), bu_tiles(bb)], axis=2)

    def rd_block(cp):
        t = cp.reshape(depth, N_RD, RD_GROUPS, SSM_GROUP_CH, SSM_STATE)
        t = jnp.einsum('lmgcp,gh->lmgphc', t, jnp.eye(RD_GROUPS, dtype=_F32))
        return t.reshape(depth, N_RD, RD_K, RD_N)

    def rd_tiles(cp, cap):
        return jnp.concatenate([rd_block(cp), rd_block(cap)], axis=-1)

    n_dt = WIDTH_A // MXU_TILE
    g_dt = MXU_TILE // SSM_GROUP_CH
    wd = jnp.einsum('lmgcd,gh->lmgdhc', direct.reshape(depth, n_dt, g_dt, SSM_GROUP_CH, SSM_GROUP_CH),
                    jnp.eye(g_dt, dtype=_F32)).reshape(depth, n_dt, MXU_TILE, MXU_TILE)
    n_pp = len(POOL_WINDOWS) // 2
    wpool2 = jnp.einsum('lmkcd,kj->lmkcjd', w_pool.reshape(depth, n_pp, 2, POOL_GROUP_CH, POOL_GROUP_CH),
                        jnp.eye(2, dtype=_F32)).reshape(depth, n_pp, 2 * POOL_GROUP_CH, 2 * POOL_GROUP_CH)
    wp = jnp.concatenate([pair_tiles(ab_re, bb_re), pair_tiles(ab_im, bb_im)], axis=1)
    row = lambda v: v.reshape(depth, 1, -1).astype(_F32)
    g_fin = jnp.broadcast_to(g_final.reshape(1, 1, -1), (depth, 1, D_MODEL)).astype(_F32)
    w_in_g = g_mix.astype(_F32)[:, :, None] * w_in
    w_ff1_g = g_ffn.astype(_F32)[:, :, None] * w_ff1
    return (g_fin, w_in_g.astype(_BF16), wp.astype(_BF16),
            row(a2_re), row(a2_im), rd_tiles(c_re, ca_re).astype(_BF16),
            rd_tiles(-c_im, -ca_im).astype(_BF16), wd.astype(_BF16),
            row(d), w_glu.astype(_BF16), wpool2.astype(_BF16), row(pool_scale),
            w_out.astype(_BF16), w_ff1_g.astype(_BF16), w_ff2.astype(_BF16))


def _hist_to_segments(hist, nb):
    b, h, c = hist.shape
    return jnp.swapaxes(hist.reshape(b // nb, nb, h, c), 1, 2).reshape(b // nb, h * nb, c)


def _hist_from_segments(hist_seg, nb):
    g, _, c = hist_seg.shape
    return jnp.swapaxes(hist_seg.reshape(g, POOL_HIST, nb, c), 1, 2).reshape(g * nb, POOL_HIST, c)


def kernel(x_prompt, x_sample, state_ssm_re, state_ssm_im, state_pool, g_mix, g_ffn, w_in,
           ssm_lambda_re, ssm_lambda_im, ssm_log_dt, ssm_b_re, ssm_b_im, ssm_c_re, ssm_c_im,
           ssm_d, w_glu, w_pool, pool_scale, w_out, w_ff1, w_ff2, g_final):
    nb = x_prompt.shape[0]
    bs = x_sample.shape[0]
    weights = _prep_weights(g_mix, g_ffn, g_final, w_in, ssm_lambda_re, ssm_lambda_im, ssm_log_dt,
                            ssm_b_re, ssm_b_im, ssm_c_re, ssm_c_im, ssm_d, w_glu, w_pool,
                            pool_scale, w_out, w_ff1, w_ff2)
    xp, xs = x_prompt, x_sample
    zero_state = jnp.zeros((1, nb, STATE_W), _F32)
    zero_hist = jnp.zeros((1, POOL_HIST * nb, WIDTH_B), _F32)
    states = lambda v: v.reshape(bs // nb, nb, STATE_W).astype(_F32)
    outs = {k: [] for k in ("p_re", "p_im", "p_pool", "s_re", "s_im", "s_pool")}
    for l in range(DEPTH):
        xp, xs, h_re, h_im, hist = _layer_call(
            xp, xs,
            jnp.concatenate([zero_state, states(state_ssm_re[l])]),
            jnp.concatenate([zero_state, states(state_ssm_im[l])]),
            jnp.concatenate([zero_hist, _hist_to_segments(state_pool[l].astype(_F32), nb)]),
            weights, l, final_norm=(l == DEPTH - 1))
        outs["p_re"].append(h_re[0].reshape(nb, SSM_GROUPS, SSM_STATE))
        outs["p_im"].append(h_im[0].reshape(nb, SSM_GROUPS, SSM_STATE))
        outs["p_pool"].append(_hist_from_segments(hist[:1], nb))
        outs["s_re"].append(h_re[1:].reshape(bs, SSM_GROUPS, SSM_STATE))
        outs["s_im"].append(h_im[1:].reshape(bs, SSM_GROUPS, SSM_STATE))
        outs["s_pool"].append(_hist_from_segments(hist[1:], nb))
    return (xp, xs,
            jnp.stack(outs["p_re"]), jnp.stack(outs["p_im"]), jnp.stack(outs["p_pool"]),
            jnp.stack(outs["s_re"]), jnp.stack(outs["s_im"]), jnp.stack(outs["s_pool"]))
```

```python
import functools

import jax
import jax.numpy as jnp
from jax import lax
from jax.experimental import pallas as pl
from jax.experimental.pallas import tpu as pltpu

D_MODEL = 1024
DEPTH = 2
PAST_LEN = 4096
WIDTH_A = 512
WIDTH_B = 512
MIX_WIDTH = WIDTH_A + WIDTH_B
SSM_GROUP_CH = 16
SSM_GROUPS = WIDTH_A // SSM_GROUP_CH
SSM_STATE = 64
STATE_W = SSM_GROUPS * SSM_STATE
POOL_WINDOWS = (2, 4, 8, 16)
POOL_GROUP_CH = WIDTH_B // len(POOL_WINDOWS)
POOL_HIST = max(POOL_WINDOWS) - 1
D_FF = 4 * D_MODEL
RMS_EPS = 1e-5

TILE_ROWS = 512
FF_CHUNK = 1024
OUT_CHUNK = 256
MXU_TILE = 256
BU_TILE = MXU_TILE
BU_GROUPS = BU_TILE // SSM_STATE
BU_K = 128
N_BU_HALF = STATE_W // BU_TILE
RD_GROUPS = 8
RD_K = RD_GROUPS * SSM_STATE
RD_N = RD_GROUPS * SSM_GROUP_CH
N_RD = SSM_GROUPS // RD_GROUPS
SCAN_LANES = 128
VMEM_LIMIT_BYTES = 60 * 1024 * 1024

_BF16 = jnp.bfloat16
_F32 = jnp.float32


def _rms_scale(x):
    ms = jnp.mean(x * x, axis=-1, keepdims=True)
    return x * lax.rsqrt(ms + RMS_EPS)


def _bu_k_block(n):
    return (n % N_BU_HALF) * BU_GROUPS * SSM_GROUP_CH // BU_K


def _layer_kernel(nb, tt, n_p, per_group, n_tiles, final_norm,
                  xp_hbm, xs_hbm, h0re_ref, h0im_ref, hist0_ref,
                  gfin_ref, win_ref, wp_ref, a2re_ref, a2im_ref,
                  wrre_ref, wrim_ref, wd_ref, d_ref, wglu_ref, wpool_ref, pscale_ref,
                  wout_ref, wff1_ref, wff2_ref,
                  yp_hbm, ys_hbm, hre_out, him_out, hist_out,
                  hre, him, ext, hs, x1_buf, hf_buf, mid_buf, xbuf, ybuf, sem_in, sem_out):
    rows = nb * tt
    half = rows // 2
    hist_rows = POOL_HIST * nb
    i = pl.program_id(0)
    live = i < n_tiles

    slot = i % 2
    yslot = 1 - slot
    tile_m = jnp.minimum(i, n_tiles - 1)
    samp_m = jnp.maximum(tile_m - n_p, 0)
    is_prompt_m = tile_m < n_p
    seg = jnp.where(is_prompt_m, 0, 1 + samp_m // per_group)
    first_of_seg = jnp.where(is_prompt_m, tile_m == 0, samp_m % per_group == 0)

    def tile_copies(tile, s, buf, sem, p_hbm, s_hbm, to_vmem, op):
        def run(hbm, row0, t0):
            for b in range(nb):
                win = hbm.at[row0 + b, pl.ds(pl.multiple_of(t0, tt), tt), :]
                vm = buf.at[s, :, b, :]
                c = pltpu.make_async_copy(win, vm, sem.at[s]) if to_vmem else \
                    pltpu.make_async_copy(vm, win, sem.at[s])
                getattr(c, op)()

        @pl.when(tile < n_p)
        def _():
            run(p_hbm, 0, tile * tt)

        @pl.when(tile >= n_p)
        def _():
            j = tile - n_p
            run(s_hbm, (j // per_group) * nb, (j % per_group) * tt)

    def x_tile(tile, s, op):
        tile_copies(tile, s, xbuf, sem_in, xp_hbm, xs_hbm, True, op)

    def y_tile(tile, s, op):
        tile_copies(tile, s, ybuf, sem_out, yp_hbm, ys_hbm, False, op)

    @pl.when(i == 0)
    def _():
        x_tile(i, 0, "start")
        x1_buf[...] = jnp.zeros((rows, D_MODEL), _F32)
        hf_buf[...] = jnp.zeros((rows, D_MODEL), _BF16)

    @pl.when(first_of_seg & live)
    def _():
        hre[...] = h0re_ref[seg]
        him[...] = h0im_ref[seg]
        ext[0:hist_rows, :] = hist0_ref[seg]

    @pl.when(i + 1 < n_tiles)
    def _():
        x_tile(i + 1, yslot, "start")

    @pl.when(i < n_tiles)
    def _():
        x_tile(i, slot, "wait")

    @pl.when(i >= 3)
    def _():
        y_tile(i - 3, yslot, "wait")

    x1_prev = x1_buf[...]
    hf = hf_buf[...]

    def mlp_up(c):
        cs = slice(c * FF_CHUNK, (c + 1) * FF_CHUNK)
        mid = jnp.dot(hf, wff1_ref[:, cs], preferred_element_type=_F32)
        mid_buf[:, cs] = jnp.square(jnp.maximum(mid, 0.0)).astype(_BF16)

    def mlp_down(n):
        ns = slice(n * OUT_CHUNK, (n + 1) * OUT_CHUNK)
        return x1_prev[:, ns] + jnp.dot(mid_buf[...], wff2_ref[:, ns], preferred_element_type=_F32)

    mlp_up(0)
    mlp_up(1)

    x = xbuf[slot].reshape(rows, D_MODEL)
    hn = _rms_scale(x).astype(_BF16)
    proj = jnp.dot(hn, win_ref[...], preferred_element_type=_F32)
    ua = proj[:, :WIDTH_A]
    ub = proj[:, WIDTH_A:]
    ua_pairs = ua.reshape(tt // 2, 2, nb, WIDTH_A)
    ue_bf = ua_pairs[:, 0].reshape(half, WIDTH_A).astype(_BF16)
    uo_bf = ua_pairs[:, 1].reshape(half, WIDTH_A).astype(_BF16)

    for n in range(2 * N_BU_HALF):
        ks = slice(_bu_k_block(n) * BU_K, (_bu_k_block(n) + 1) * BU_K)
        lhs = jnp.concatenate([ue_bf[:, ks], uo_bf[:, ks]], axis=1)
        hs[nb:nb + half, n * BU_TILE:(n + 1) * BU_TILE] = jnp.dot(
            lhs, wp_ref[n], preferred_element_type=_F32)

    lc = SCAN_LANES
    n_lane_chunks = STATE_W // lc
    always = i >= 0

    def scan_lane_chunk(c, chain):
        re_sl = slice(c * lc, (c + 1) * lc)
        im_sl = slice(STATE_W + c * lc, STATE_W + (c + 1) * lc)
        a_re = jnp.broadcast_to(a2re_ref[:, re_sl], (nb, lc))
        a_im = jnp.broadcast_to(a2im_ref[:, re_sl], (nb, lc))
        h_r0 = hre[:, re_sl]
        h_i0 = him[:, re_sl]
        hs[0:nb, re_sl] = h_r0
        hs[0:nb, im_sl] = h_i0
        h_r, h_i = h_r0, h_i0
        if chain is not None:
            h_r = jnp.where(always, h_r, chain)
        for k in range(tt // 2):
            rs = slice((k + 1) * nb, (k + 2) * nb)
            n_r = a_re * h_r - a_im * h_i + hs[rs, re_sl]
            n_i = a_re * h_i + a_im * h_r + hs[rs, im_sl]
            hs[rs, re_sl] = n_r
            hs[rs, im_sl] = n_i
            h_r, h_i = n_r, n_i
        hre[:, re_sl] = jnp.where(live, h_r, h_r0)
        him[:, re_sl] = jnp.where(live, h_i, h_i0)
        return h_r

    chain = None
    for c in range(n_lane_chunks // 2):
        chain = scan_lane_chunk(c, chain)

    mlp_up(2)
    mlp_up(3)

    for c in range(n_lane_chunks // 2, n_lane_chunks):
        chain = scan_lane_chunk(c, chain)
    hre_out[seg] = hre[...]
    him_out[seg] = him[...]
    y_odd, y_even = [], []
    for m in range(N_RD):
        h_re = hs[:, m * RD_K:(m + 1) * RD_K].astype(_BF16)
        h_im = hs[:, STATE_W + m * RD_K:STATE_W + (m + 1) * RD_K].astype(_BF16)
        out = (jnp.dot(h_re, wrre_ref[m], preferred_element_type=_F32)
               + jnp.dot(h_im, wrim_ref[m], preferred_element_type=_F32))
        y_odd.append(out[nb:, :RD_N])
        y_even.append(out[:half, RD_N:])
    direct = [jnp.dot(ue_bf[:, m * MXU_TILE:(m + 1) * MXU_TILE], wd_ref[m],
                      preferred_element_type=_F32) for m in range(WIDTH_A // MXU_TILE)]
    y_even = jnp.concatenate(y_even, axis=1) + jnp.concatenate(direct, axis=1)
    y_odd = jnp.concatenate(y_odd, axis=1)
    ys = jnp.stack([y_even.reshape(tt // 2, nb, WIDTH_A), y_odd.reshape(tt // 2, nb, WIDTH_A)],
                   axis=1).reshape(rows, WIDTH_A)
    y = ys + d_ref[...] * ua
    z = jax.nn.gelu(y)
    out_a = z * jax.nn.sigmoid(jnp.dot(z.astype(_BF16), wglu_ref[...], preferred_element_type=_F32))

    y_cols = [mlp_down(0), mlp_down(1)]

    ext[hist_rows:hist_rows + rows, :] = ub
    t0 = jnp.where(is_prompt_m, tile_m * tt, PAST_LEN + (samp_m % per_group) * tt)
    pos = t0 + lax.broadcasted_iota(jnp.int32, (rows, POOL_GROUP_CH), 0) // nb
    n_seen = (pos + 1).astype(_F32)
    diffs = []
    for k, w in enumerate(POOL_WINDOWS):
        ls = slice(k * POOL_GROUP_CH, (k + 1) * POOL_GROUP_CH)
        wsum = ext[(POOL_HIST - (w - 1)) * nb:hist_rows + rows, ls]
        shift = 1
        while shift < w:
            wsum = wsum[shift * nb:] + wsum[:-shift * nb]
            shift *= 2
        cnt = jnp.minimum(n_seen, float(w))
        diffs.append((wsum / cnt - ub[:, ls]).astype(_BF16))
    yb_parts = [jnp.dot(jnp.concatenate(diffs[2 * m:2 * m + 2], axis=1), wpool_ref[m],
                        preferred_element_type=_F32) for m in range(len(POOL_WINDOWS) // 2)]
    yb = jnp.concatenate(yb_parts, axis=1) * pscale_ref[...]
    new_hist = jnp.where(live, ext[rows:rows + hist_rows, :], ext[0:hist_rows, :])
    hist_out[seg] = new_hist
    ext[0:hist_rows, :] = new_hist

    mixed = jnp.concatenate([out_a, yb], axis=1)
    x1 = x + jnp.dot(mixed.astype(_BF16), wout_ref[...], preferred_element_type=_F32)

    y_cols += [mlp_down(2), mlp_down(3)]
    acc = jnp.concatenate(y_cols, axis=1)
    if final_norm:
        acc = _rms_scale(acc) * gfin_ref[...]
    ybuf[yslot] = acc.reshape(tt, nb, D_MODEL)
    x1_buf[...] = x1
    hf_buf[...] = _rms_scale(x1).astype(_BF16)

    @pl.when(i >= 1)
    def _():
        y_tile(i - 1, yslot, "start")

    @pl.when(i == n_tiles)
    def _():
        y_tile(i - 2, slot, "wait")
        y_tile(i - 1, yslot, "wait")


def _layer_call(xp, xs, h0_re, h0_im, hist0, weights, layer, *, final_norm):
    nb, seq_p, _ = xp.shape
    bs, seq_s, _ = xs.shape
    tt = TILE_ROWS // nb
    assert seq_p % tt == 0 and seq_s % tt == 0 and bs % nb == 0 and tt >= POOL_HIST and tt % 8 == 0
    assert D_FF // FF_CHUNK == 4 and D_MODEL // OUT_CHUNK == 4, "the kernel body is written for 4 + 4 MLP chunks"
    n_p = seq_p // tt
    per_group = seq_s // tt
    n_seg = 1 + bs // nb
    n_tiles = n_p + (bs // nb) * per_group
    assert n_tiles >= 3
    hist_rows = POOL_HIST * nb

    def resident(w):
        zeros = (0,) * (w.ndim - 1)
        return pl.BlockSpec((None,) + w.shape[1:], lambda i: (layer,) + zeros,
                            pipeline_mode=pl.Buffered(1))

    whole = pl.BlockSpec(memory_space=pltpu.VMEM)
    hbm = pl.BlockSpec(memory_space=pl.ANY)
    out_shape = (
        jax.ShapeDtypeStruct(xp.shape, _F32),
        jax.ShapeDtypeStruct(xs.shape, _F32),
        jax.ShapeDtypeStruct((n_seg, nb, STATE_W), _F32),
        jax.ShapeDtypeStruct((n_seg, nb, STATE_W), _F32),
        jax.ShapeDtypeStruct((n_seg, hist_rows, WIDTH_B), _F32),
    )
    return pl.pallas_call(
        functools.partial(_layer_kernel, nb, tt, n_p, per_group, n_tiles, final_norm),
        grid=(n_tiles + 1,),
        in_specs=[hbm, hbm] + [resident(w) for w in (h0_re, h0_im, hist0) + tuple(weights)],
        out_specs=(hbm, hbm, whole, whole, whole),
        out_shape=out_shape,
        scratch_shapes=[
            pltpu.VMEM((nb, STATE_W), _F32),
            pltpu.VMEM((nb, STATE_W), _F32),
            pltpu.VMEM((hist_rows + TILE_ROWS, WIDTH_B), _F32),
            pltpu.VMEM((TILE_ROWS // 2 + nb, 2 * STATE_W), _F32),
            pltpu.VMEM((TILE_ROWS, D_MODEL), _F32),
            pltpu.VMEM((TILE_ROWS, D_MODEL), _BF16),
            pltpu.VMEM((TILE_ROWS, D_FF), _BF16),
            pltpu.VMEM((2, tt, nb, D_MODEL), _F32),
            pltpu.VMEM((2, tt, nb, D_MODEL), _F32),
            pltpu.SemaphoreType.DMA((2,)),
            pltpu.SemaphoreType.DMA((2,)),
        ],
        compiler_params=pltpu.CompilerParams(
            dimension_semantics=("arbitrary",), vmem_limit_bytes=VMEM_LIMIT_BYTES),
    )(xp, xs, h0_re, h0_im, hist0, *weights)


def _prep_weights(g_mix, g_ffn, g_final, w_in, lam_re, lam_im, log_dt, b_re, b_im, c_re, c_im,
                  d, w_glu, w_pool, pool_scale, w_out, w_ff1, w_ff2):
    depth = w_in.shape[0]
    dt = jnp.exp(log_dt)[..., None]
    mag = jnp.exp(lam_re * dt)
    a_re = mag * jnp.cos(lam_im * dt)
    a_im = mag * jnp.sin(lam_im * dt)
    den = lam_re * lam_re + lam_im * lam_im
    q_re = ((a_re - 1.0) * lam_re + a_im * lam_im) / den
    q_im = (a_im * lam_re - (a_re - 1.0) * lam_im) / den
    bb_re = q_re[..., None] * b_re - q_im[..., None] * b_im
    bb_im = q_re[..., None] * b_im + q_im[..., None] * b_re
    ab_re = a_re[..., None] * bb_re - a_im[..., None] * bb_im
    ab_im = a_re[..., None] * bb_im + a_im[..., None] * bb_re
    a2_re = a_re * a_re - a_im * a_im
    a2_im = 2.0 * a_re * a_im
    ca_re = c_re * a_re[:, :, None, :] - c_im * a_im[:, :, None, :]
    ca_im = c_re * a_im[:, :, None, :] + c_im * a_re[:, :, None, :]
    direct = (jnp.einsum('lgcp,lgpd->lgcd', c_re, bb_re)
              - jnp.einsum('lgcp,lgpd->lgcd', c_im, bb_im))

    eye_bu = jnp.eye(BU_GROUPS, dtype=_F32)
    half_sel = (jnp.arange(N_BU_HALF)[:, None] % 2 == jnp.arange(2)[None, :]).astype(_F32)
    x = jnp.swapaxes(jnp.stack([ab_re, bb_re, ab_im, bb_im]), -1, -2)
    x = x.reshape(4, depth, N_BU_HALF, BU_GROUPS, SSM_GROUP_CH, SSM_STATE)
    x = jnp.einsum('xlnqcp,qr,nh->xlnhqcrp', x, eye_bu, half_sel).reshape(4, depth, N_BU_HALF, BU_K, BU_TILE)
    wp = jnp.concatenate([jnp.concatenate([x[0], x[1]], axis=2),
                          jnp.concatenate([x[2], x[3]], axis=2)], axis=1)

    y = jnp.stack([c_re, ca_re, -c_im, -ca_im]).reshape(4, depth, N_RD, RD_GROUPS, SSM_GROUP_CH, SSM_STATE)
    y = jnp.einsum('xlmgcp,gh->xlmgphc', y, jnp.eye(RD_GROUPS, dtype=_F32)).reshape(4, depth, N_RD, RD_K, RD_N)
    wr_re = jnp.concatenate([y[0], y[1]], axis=-1)
    wr_im = jnp.concatenate([y[2], y[3]], axis=-1)

    n_dt = WIDTH_A // MXU_TILE
    g_dt = MXU_TILE // SSM_GROUP_CH
    wd = jnp.einsum('lmgcd,gh->lmgdhc', direct.reshape(depth, n_dt, g_dt, SSM_GROUP_CH, SSM_GROUP_CH),
                    jnp.eye(g_dt, dtype=_F32)).reshape(depth, n_dt, MXU_TILE, MXU_TILE)
    n_pp = len(POOL_WINDOWS) // 2
    wpool2 = jnp.einsum('lmkcd,kj->lmkcjd', w_pool.reshape(depth, n_pp, 2, POOL_GROUP_CH, POOL_GROUP_CH),
                        jnp.eye(2, dtype=_F32)).reshape(depth, n_pp, 2 * POOL_GROUP_CH, 2 * POOL_GROUP_CH)
    row = lambda v: v.reshape(depth, 1, -1).astype(_F32)
    g_fin = jnp.broadcast_to(g_final.reshape(1, 1, -1), (depth, 1, D_MODEL)).astype(_F32)
    w_in_g = g_mix.astype(_F32)[:, :, None] * w_in
    w_ff1_g = g_ffn.astype(_F32)[:, :, None] * w_ff1
    return (g_fin, w_in_g.astype(_BF16), wp.astype(_BF16),
            row(a2_re), row(a2_im), wr_re.astype(_BF16), wr_im.astype(_BF16), wd.astype(_BF16),
            row(d), w_glu.astype(_BF16), wpool2.astype(_BF16), row(pool_scale),
            w_out.astype(_BF16), w_ff1_g.astype(_BF16), w_ff2.astype(_BF16))


def _hist_to_segments(hist, nb):
    b, h, c = hist.shape
    return jnp.swapaxes(hist.reshape(b // nb, nb, h, c), 1, 2).reshape(b // nb, h * nb, c)


def _hist_from_segments(hist_seg, nb):
    g, _, c = hist_seg.shape
    return jnp.swapaxes(hist_seg.reshape(g, POOL_HIST, nb, c), 1, 2).reshape(g * nb, POOL_HIST, c)


def kernel(x_prompt, x_sample, state_ssm_re, state_ssm_im, state_pool, g_mix, g_ffn, w_in,
           ssm_lambda_re, ssm_lambda_im, ssm_log_dt, ssm_b_re, ssm_b_im, ssm_c_re, ssm_c_im,
           ssm_d, w_glu, w_pool, pool_scale, w_out, w_ff1, w_ff2, g_final):
    nb = x_prompt.shape[0]
    bs = x_sample.shape[0]
    weights = _prep_weights(g_mix, g_ffn, g_final, w_in, ssm_lambda_re, ssm_lambda_im, ssm_log_dt,
                            ssm_b_re, ssm_b_im, ssm_c_re, ssm_c_im, ssm_d, w_glu, w_pool,
                            pool_scale, w_out, w_ff1, w_ff2)
    xp, xs = x_prompt, x_sample
    seg_states = lambda v: jnp.pad(v.reshape(DEPTH, bs // nb, nb, STATE_W).astype(_F32),
                                   ((0, 0), (1, 0), (0, 0), (0, 0)))
    h0_re, h0_im = seg_states(state_ssm_re), seg_states(state_ssm_im)
    hist0 = jnp.pad(_hist_to_segments(state_pool.reshape((DEPTH * bs,) + state_pool.shape[2:]).astype(_F32), nb)
                    .reshape(DEPTH, bs // nb, POOL_HIST * nb, WIDTH_B), ((0, 0), (1, 0), (0, 0), (0, 0)))
    h_re, h_im, hist = [], [], []
    for l in range(DEPTH):
        xp, xs, hr, hi, hh = _layer_call(xp, xs, h0_re, h0_im, hist0, weights, l,
                                         final_norm=(l == DEPTH - 1))
        h_re.append(hr)
        h_im.append(hi)
        hist.append(hh)
    h_re, h_im, hist = jnp.stack(h_re), jnp.stack(h_im), jnp.stack(hist)
    state_out = lambda h, lo, n: h[:, lo:].reshape(DEPTH, -1, SSM_GROUPS, SSM_STATE)[:, :n]
    pools = _hist_from_segments(hist.reshape((-1,) + hist.shape[2:]), nb).reshape(
        DEPTH, nb + bs, POOL_HIST, WIDTH_B)
    return (xp, xs,
            state_out(h_re, 0, nb), state_out(h_im, 0, nb), pools[:, :nb],
            state_out(h_re, 1, bs), state_out(h_im, 1, bs), pools[:, nb:])
```

```python
import functools

import jax
import jax.numpy as jnp
from jax import lax
from jax.experimental import pallas as pl
from jax.experimental.pallas import tpu as pltpu

D_MODEL = 1024
DEPTH = 2
PAST_LEN = 4096
WIDTH_A = 512
WIDTH_B = 512
MIX_WIDTH = WIDTH_A + WIDTH_B
SSM_GROUP_CH = 16
SSM_GROUPS = WIDTH_A // SSM_GROUP_CH
SSM_STATE = 64
STATE_W = SSM_GROUPS * SSM_STATE
POOL_WINDOWS = (2, 4, 8, 16)
POOL_GROUP_CH = WIDTH_B // len(POOL_WINDOWS)
POOL_HIST = max(POOL_WINDOWS) - 1
D_FF = 4 * D_MODEL
RMS_EPS = 1e-5

TILE_ROWS = 512
FF_CHUNK = 1024
OUT_CHUNK = 256
MXU_TILE = 256
BU_TILE = MXU_TILE
BU_GROUPS = BU_TILE // SSM_STATE
BU_K = 128
N_BU_HALF = STATE_W // BU_TILE
RD_GROUPS = 8
RD_K = RD_GROUPS * SSM_STATE
RD_N = RD_GROUPS * SSM_GROUP_CH
N_RD = SSM_GROUPS // RD_GROUPS
SCAN_LANES = 128
VMEM_LIMIT_BYTES = 60 * 1024 * 1024

_BF16 = jnp.bfloat16
_F32 = jnp.float32


def _rms_scale(x):
    ms = jnp.mean(x * x, axis=-1, keepdims=True)
    return x * lax.rsqrt(ms + RMS_EPS)


def _bu_k_block(n):
    return (n % N_BU_HALF) * BU_GROUPS * SSM_GROUP_CH // BU_K


def _layer_kernel(nb, tt, n_p, per_group, n_tiles, final_norm,
                  xp_hbm, xs_hbm, h0re_ref, h0im_ref, hist0_ref,
                  gfin_ref, win_ref, wp_ref, a2re_ref, a2im_ref,
                  wrre_ref, wrim_ref, wd_ref, d_ref, wglu_ref, wpool_ref, pscale_ref,
                  wout_ref, wff1_ref, wff2_ref,
                  yp_hbm, ys_hbm, hre_out, him_out, hist_out,
                  hre, him, ext, hs, x1_buf, hf_buf, mid_buf, xbuf, ybuf, sem_in, sem_out):
    rows = nb * tt
    half = rows // 2
    hist_rows = POOL_HIST * nb
    i = pl.program_id(0)
    live = i < n_tiles

    slot = i % 2
    yslot = 1 - slot
    tile_m = jnp.minimum(i, n_tiles - 1)
    samp_m = jnp.maximum(tile_m - n_p, 0)
    is_prompt_m = tile_m < n_p
    seg = jnp.where(is_prompt_m, 0, 1 + samp_m // per_group)
    first_of_seg = jnp.where(is_prompt_m, tile_m == 0, samp_m % per_group == 0)

    def tile_copies(tile, s, buf, sem, p_hbm, s_hbm, to_vmem, op):
        def run(hbm, row0, t0):
            for b in range(nb):
                win = hbm.at[row0 + b, pl.ds(pl.multiple_of(t0, tt), tt), :]
                vm = buf.at[s, :, b, :]
                c = pltpu.make_async_copy(win, vm, sem.at[s]) if to_vmem else \
                    pltpu.make_async_copy(vm, win, sem.at[s])
                getattr(c, op)()

        @pl.when(tile < n_p)
        def _():
            run(p_hbm, 0, tile * tt)

        @pl.when(tile >= n_p)
        def _():
            j = tile - n_p
            run(s_hbm, (j // per_group) * nb, (j % per_group) * tt)

    def x_tile(tile, s, op):
        tile_copies(tile, s, xbuf, sem_in, xp_hbm, xs_hbm, True, op)

    def y_tile(tile, s, op):
        tile_copies(tile, s, ybuf, sem_out, yp_hbm, ys_hbm, False, op)

    @pl.when(i == 0)
    def _():
        x_tile(i, 0, "start")

    @pl.when(first_of_seg & live)
    def _():
        hre[...] = h0re_ref[seg]
        him[...] = h0im_ref[seg]
        ext[0:hist_rows, :] = hist0_ref[seg]

    @pl.when(i + 1 < n_tiles)
    def _():
        x_tile(i + 1, yslot, "start")

    @pl.when(i < n_tiles)
    def _():
        x_tile(i, slot, "wait")

    @pl.when(i >= 3)
    def _():
        y_tile(i - 3, yslot, "wait")

    def step(run_mlp, run_mixer):
        if run_mlp:
            x1_prev = x1_buf[...]
            hf = hf_buf[...]

            def mlp_up(c):
                cs = slice(c * FF_CHUNK, (c + 1) * FF_CHUNK)
                mid = jnp.dot(hf, wff1_ref[:, cs], preferred_element_type=_F32)
                mid_buf[:, cs] = jnp.square(jnp.maximum(mid, 0.0)).astype(_BF16)

            def mlp_down(n):
                ns = slice(n * OUT_CHUNK, (n + 1) * OUT_CHUNK)
                return x1_prev[:, ns] + jnp.dot(mid_buf[...], wff2_ref[:, ns], preferred_element_type=_F32)

            mlp_up(0)
            mlp_up(1)

        if run_mixer:
            x = xbuf[slot].reshape(rows, D_MODEL)
            hn = _rms_scale(x).astype(_BF16)
            proj = jnp.dot(hn, win_ref[...], preferred_element_type=_F32)
            ua = proj[:, :WIDTH_A]
            ub = proj[:, WIDTH_A:]
            ua_pairs = ua.reshape(tt // 2, 2, nb, WIDTH_A)
            ue_bf = ua_pairs[:, 0].reshape(half, WIDTH_A).astype(_BF16)
            uo_bf = ua_pairs[:, 1].reshape(half, WIDTH_A).astype(_BF16)

            for n in range(2 * N_BU_HALF):
                ks = slice(_bu_k_block(n) * BU_K, (_bu_k_block(n) + 1) * BU_K)
                lhs = jnp.concatenate([ue_bf[:, ks], uo_bf[:, ks]], axis=1)
                hs[nb:nb + half, n * BU_TILE:(n + 1) * BU_TILE] = jnp.dot(
                    lhs, wp_ref[n], preferred_element_type=_F32)

            lc = SCAN_LANES
            n_lane_chunks = STATE_W // lc
            always = i >= 0

            def scan_lane_chunk(c, chain):
                re_sl = slice(c * lc, (c + 1) * lc)
                im_sl = slice(STATE_W + c * lc, STATE_W + (c + 1) * lc)
                a_re = jnp.broadcast_to(a2re_ref[:, re_sl], (nb, lc))
                a_im = jnp.broadcast_to(a2im_ref[:, re_sl], (nb, lc))
                h_r0 = hre[:, re_sl]
                h_i0 = him[:, re_sl]
                hs[0:nb, re_sl] = h_r0
                hs[0:nb, im_sl] = h_i0
                h_r, h_i = h_r0, h_i0
                if chain is not None:
                    h_r = jnp.where(always, h_r, chain)
                for k in range(tt // 2):
                    rs = slice((k + 1) * nb, (k + 2) * nb)
                    n_r = a_re * h_r - a_im * h_i + hs[rs, re_sl]
                    n_i = a_re * h_i + a_im * h_r + hs[rs, im_sl]
                    hs[rs, re_sl] = n_r
                    hs[rs, im_sl] = n_i
                    h_r, h_i = n_r, n_i
                hre[:, re_sl] = h_r
                him[:, re_sl] = h_i
                return h_r

            chain = None
            for c in range(n_lane_chunks // 2):
                chain = scan_lane_chunk(c, chain)

        if run_mlp:
            mlp_up(2)
            mlp_up(3)

        if run_mixer:
            for c in range(n_lane_chunks // 2, n_lane_chunks):
                chain = scan_lane_chunk(c, chain)
            hre_out[seg] = hre[...]
            him_out[seg] = him[...]
            y_odd, y_even = [], []
            for m in range(N_RD):
                h_re = hs[:, m * RD_K:(m + 1) * RD_K].astype(_BF16)
                h_im = hs[:, STATE_W + m * RD_K:STATE_W + (m + 1) * RD_K].astype(_BF16)
                out = (jnp.dot(h_re, wrre_ref[m], preferred_element_type=_F32)
                       + jnp.dot(h_im, wrim_ref[m], preferred_element_type=_F32))
                y_odd.append(out[nb:, :RD_N])
                y_even.append(out[:half, RD_N:])
            direct = [jnp.dot(ue_bf[:, m * MXU_TILE:(m + 1) * MXU_TILE], wd_ref[m],
                              preferred_element_type=_F32) for m in range(WIDTH_A // MXU_TILE)]
            y_even = jnp.concatenate(y_even, axis=1) + jnp.concatenate(direct, axis=1)
            y_odd = jnp.concatenate(y_odd, axis=1)
            ys = jnp.stack([y_even.reshape(tt // 2, nb, WIDTH_A), y_odd.reshape(tt // 2, nb, WIDTH_A)],
                           axis=1).reshape(rows, WIDTH_A)
            y = ys + d_ref[...] * ua
            z = jax.nn.gelu(y)
            out_a = z * jax.nn.sigmoid(jnp.dot(z.astype(_BF16), wglu_ref[...], preferred_element_type=_F32))

        if run_mlp:
            y_cols = [mlp_down(0), mlp_down(1)]

        if run_mixer:
            ext[hist_rows:hist_rows + rows, :] = ub
            t0 = jnp.where(is_prompt_m, tile_m * tt, PAST_LEN + (samp_m % per_group) * tt)
            pos = t0 + lax.broadcasted_iota(jnp.int32, (rows, POOL_GROUP_CH), 0) // nb
            n_seen = (pos + 1).astype(_F32)
            diffs = []
            for k, w in enumerate(POOL_WINDOWS):
                ls = slice(k * POOL_GROUP_CH, (k + 1) * POOL_GROUP_CH)
                wsum = ext[(POOL_HIST - (w - 1)) * nb:hist_rows + rows, ls]
                shift = 1
                while shift < w:
                    wsum = wsum[shift * nb:] + wsum[:-shift * nb]
                    shift *= 2
                cnt = jnp.minimum(n_seen, float(w))
                diffs.append((wsum / cnt - ub[:, ls]).astype(_BF16))
            yb_parts = [jnp.dot(jnp.concatenate(diffs[2 * m:2 * m + 2], axis=1), wpool_ref[m],
                                preferred_element_type=_F32) for m in range(len(POOL_WINDOWS) // 2)]
            yb = jnp.concatenate(yb_parts, axis=1) * pscale_ref[...]
            new_hist = ext[rows:rows + hist_rows, :]
            hist_out[seg] = new_hist
            ext[0:hist_rows, :] = new_hist

            mixed = jnp.concatenate([out_a, yb], axis=1)
            x1 = x + jnp.dot(mixed.astype(_BF16), wout_ref[...], preferred_element_type=_F32)

        if run_mlp:
            y_cols += [mlp_down(2), mlp_down(3)]
            acc = jnp.concatenate(y_cols, axis=1)
            if final_norm:
                acc = _rms_scale(acc) * gfin_ref[...]
            ybuf[yslot] = acc.reshape(tt, nb, D_MODEL)

        if run_mixer:
            x1_buf[...] = x1
            hf_buf[...] = _rms_scale(x1).astype(_BF16)

    @pl.when(i == 0)
    def _():
        step(False, True)

    @pl.when((i > 0) & (i < n_tiles))
    def _():
        step(True, True)

    @pl.when(i == n_tiles)
    def _():
        step(True, False)

    @pl.when(i >= 1)
    def _():
        y_tile(i - 1, yslot, "start")

    @pl.when(i == n_tiles)
    def _():
        y_tile(i - 2, slot, "wait")
        y_tile(i - 1, yslot, "wait")


def _layer_call(xp, xs, h0_re, h0_im, hist0, weights, layer, *, final_norm):
    nb, seq_p, _ = xp.shape
    bs, seq_s, _ = xs.shape
    tt = TILE_ROWS // nb
    assert seq_p % tt == 0 and seq_s % tt == 0 and bs % nb == 0 and tt >= POOL_HIST and tt % 8 == 0
    assert D_FF // FF_CHUNK == 4 and D_MODEL // OUT_CHUNK == 4, "the kernel body is written for 4 + 4 MLP chunks"
    n_p = seq_p // tt
    per_group = seq_s // tt
    n_seg = 1 + bs // nb
    n_tiles = n_p + (bs // nb) * per_group
    assert n_tiles >= 3
    hist_rows = POOL_HIST * nb

    def resident(w):
        zeros = (0,) * (w.ndim - 1)
        return pl.BlockSpec((None,) + w.shape[1:], lambda i: (layer,) + zeros,
                            pipeline_mode=pl.Buffered(1))

    whole = pl.BlockSpec(memory_space=pltpu.VMEM)
    hbm = pl.BlockSpec(memory_space=pl.ANY)
    out_shape = (
        jax.ShapeDtypeStruct(xp.shape, _F32),
        jax.ShapeDtypeStruct(xs.shape, _F32),
        jax.ShapeDtypeStruct((n_seg, nb, STATE_W), _F32),
        jax.ShapeDtypeStruct((n_seg, nb, STATE_W), _F32),
        jax.ShapeDtypeStruct((n_seg, hist_rows, WIDTH_B), _F32),
    )
    return pl.pallas_call(
        functools.partial(_layer_kernel, nb, tt, n_p, per_group, n_tiles, final_norm),
        grid=(n_tiles + 1,),
        in_specs=[hbm, hbm, whole, whole, whole] + [resident(w) for w in weights],
        out_specs=(hbm, hbm, whole, whole, whole),
        out_shape=out_shape,
        scratch_shapes=[
            pltpu.VMEM((nb, STATE_W), _F32),
            pltpu.VMEM((nb, STATE_W), _F32),
            pltpu.VMEM((hist_rows + TILE_ROWS, WIDTH_B), _F32),
            pltpu.VMEM((TILE_ROWS // 2 + nb, 2 * STATE_W), _F32),
            pltpu.VMEM((TILE_ROWS, D_MODEL), _F32),
            pltpu.VMEM((TILE_ROWS, D_MODEL), _BF16),
            pltpu.VMEM((TILE_ROWS, D_FF), _BF16),
            pltpu.VMEM((2, tt, nb, D_MODEL), _F32),
            pltpu.VMEM((2, tt, nb, D_MODEL), _F32),
            pltpu.SemaphoreType.DMA((2,)),
            pltpu.SemaphoreType.DMA((2,)),
        ],
        compiler_params=pltpu.CompilerParams(
            dimension_semantics=("arbitrary",), vmem_limit_bytes=VMEM_LIMIT_BYTES),
    )(xp, xs, h0_re, h0_im, hist0, *weights)


def _prep_weights(g_mix, g_ffn, g_final, w_in, lam_re, lam_im, log_dt, b_re, b_im, c_re, c_im,
                  d, w_glu, w_pool, pool_scale, w_out, w_ff1, w_ff2):
    depth = w_in.shape[0]
    dt = jnp.exp(log_dt)[..., None]
    mag = jnp.exp(lam_re * dt)
    a_re = mag * jnp.cos(lam_im * dt)
    a_im = mag * jnp.sin(lam_im * dt)
    den = lam_re * lam_re + lam_im * lam_im
    q_re = ((a_re - 1.0) * lam_re + a_im * lam_im) / den
    q_im = (a_im * lam_re - (a_re - 1.0) * lam_im) / den
    bb_re = q_re[..., None] * b_re - q_im[..., None] * b_im
    bb_im = q_re[..., None] * b_im + q_im[..., None] * b_re
    ab_re = a_re[..., None] * bb_re - a_im[..., None] * bb_im
    ab_im = a_re[..., None] * bb_im + a_im[..., None] * bb_re
    a2_re = a_re * a_re - a_im * a_im
    a2_im = 2.0 * a_re * a_im
    ca_re = c_re * a_re[:, :, None, :] - c_im * a_im[:, :, None, :]
    ca_im = c_re * a_im[:, :, None, :] + c_im * a_re[:, :, None, :]
    direct = (jnp.einsum('lgcp,lgpd->lgcd', c_re, bb_re)
              - jnp.einsum('lgcp,lgpd->lgcd', c_im, bb_im))

    def bu_tiles(bb):
        t = jnp.swapaxes(bb, -1, -2).reshape(depth, N_BU_HALF, BU_GROUPS, SSM_GROUP_CH, SSM_STATE)
        t = jnp.einsum('lnqcp,qr->lnqcrp', t, jnp.eye(BU_GROUPS, dtype=_F32))
        t = t.reshape(depth, N_BU_HALF, BU_GROUPS * SSM_GROUP_CH, BU_TILE)
        z = jnp.zeros_like(t)
        first = (jnp.arange(N_BU_HALF) % 2 == 0)[None, :, None, None]
        return jnp.where(first, jnp.concatenate([t, z], axis=2), jnp.concatenate([z, t], axis=2))

    def pair_tiles(ab, bb):
        return jnp.concatenate([bu_tiles(ab), bu_tiles(bb)], axis=2)

    def rd_block(cp):
        t = cp.reshape(depth, N_RD, RD_GROUPS, SSM_GROUP_CH, SSM_STATE)
        t = jnp.einsum('lmgcp,gh->lmgphc', t, jnp.eye(RD_GROUPS, dtype=_F32))
        return t.reshape(depth, N_RD, RD_K, RD_N)

    def rd_tiles(cp, cap):
        return jnp.concatenate([rd_block(cp), rd_block(cap)], axis=-1)

    n_dt = WIDTH_A // MXU_TILE
    g_dt = MXU_TILE // SSM_GROUP_CH
    wd = jnp.einsum('lmgcd,gh->lmgdhc', direct.reshape(depth, n_dt, g_dt, SSM_GROUP_CH, SSM_GROUP_CH),
                    jnp.eye(g_dt, dtype=_F32)).reshape(depth, n_dt, MXU_TILE, MXU_TILE)
    n_pp = len(POOL_WINDOWS) // 2
    wpool2 = jnp.einsum('lmkcd,kj->lmkcjd', w_pool.reshape(depth, n_pp, 2, POOL_GROUP_CH, POOL_GROUP_CH),
                        jnp.eye(2, dtype=_F32)).reshape(depth, n_pp, 2 * POOL_GROUP_CH, 2 * POOL_GROUP_CH)
    wp = jnp.concatenate([pair_tiles(ab_re, bb_re), pair_tiles(ab_im, bb_im)], axis=1)
    row = lambda v: v.reshape(depth, 1, -1).astype(_F32)
    g_fin = jnp.broadcast_to(g_final.reshape(1, 1, -1), (depth, 1, D_MODEL)).astype(_F32)
    w_in_g = g_mix.astype(_F32)[:, :, None] * w_in
    w_ff1_g = g_ffn.astype(_F32)[:, :, None] * w_ff1
    return (g_fin, w_in_g.astype(_BF16), wp.astype(_BF16),
            row(a2_re), row(a2_im), rd_tiles(c_re, ca_re).astype(_BF16),
            rd_tiles(-c_im, -ca_im).astype(_BF16), wd.astype(_BF16),
            row(d), w_glu.astype(_BF16), wpool2.astype(_BF16), row(pool_scale),
            w_out.astype(_BF16), w_ff1_g.astype(_BF16), w_ff2.astype(_BF16))


def _hist_to_segments(hist, nb):
    b, h, c = hist.shape
    return jnp.swapaxes(hist.reshape(b // nb, nb, h, c), 1, 2).reshape(b // nb, h * nb, c)


def _hist_from_segments(hist_seg, nb):
    g, _, c = hist_seg.shape
    return jnp.swapaxes(hist_seg.reshape(g, POOL_HIST, nb, c), 1, 2).reshape(g * nb, POOL_HIST, c)


def kernel(x_prompt, x_sample, state_ssm_re, state_ssm_im, state_pool, g_mix, g_ffn, w_in,
           ssm_lambda_re, ssm_lambda_im, ssm_log_dt, ssm_b_re, ssm_b_im, ssm_c_re, ssm_c_im,
           ssm_d, w_glu, w_pool, pool_scale, w_out, w_ff1, w_ff2, g_final):
    nb = x_prompt.shape[0]
    bs = x_sample.shape[0]
    weights = _prep_weights(g_mix, g_ffn, g_final, w_in, ssm_lambda_re, ssm_lambda_im, ssm_log_dt,
                            ssm_b_re, ssm_b_im, ssm_c_re, ssm_c_im, ssm_d, w_glu, w_pool,
                            pool_scale, w_out, w_ff1, w_ff2)
    xp, xs = x_prompt, x_sample
    zero_state = jnp.zeros((1, nb, STATE_W), _F32)
    zero_hist = jnp.zeros((1, POOL_HIST * nb, WIDTH_B), _F32)
    states = lambda v: v.reshape(bs // nb, nb, STATE_W).astype(_F32)
    outs = {k: [] for k in ("p_re", "p_im", "p_pool", "s_re", "s_im", "s_pool")}
    for l in range(DEPTH):
        xp, xs, h_re, h_im, hist = _layer_call(
            xp, xs,
            jnp.concatenate([zero_state, states(state_ssm_re[l])]),
            jnp.concatenate([zero_state, states(state_ssm_im[l])]),
            jnp.concatenate([zero_hist, _hist_to_segments(state_pool[l].astype(_F32), nb)]),
            weights, l, final_norm=(l == DEPTH - 1))
        outs["p_re"].append(h_re[0].reshape(nb, SSM_GROUPS, SSM_STATE))
        outs["p_im"].append(h_im[0].reshape(nb, SSM_GROUPS, SSM_STATE))
        outs["p_pool"].append(_hist_from_segments(hist[:1], nb))
        outs["s_re"].append(h_re[1:].reshape(bs, SSM_GROUPS, SSM_STATE))
        outs["s_im"].append(h_im[1:].reshape(bs, SSM_GROUPS, SSM_STATE))
        outs["s_pool"].append(_hist_from_segments(hist[1:], nb))
    return (xp, xs,
            jnp.stack(outs["p_re"]), jnp.stack(outs["p_im"]), jnp.stack(outs["p_pool"]),
            jnp.stack(outs["s_re"]), jnp.stack(outs["s_im"]), jnp.stack(outs["s_pool"]))
```

```python
import functools

import jax
import jax.numpy as jnp
from jax import lax
from jax.experimental import pallas as pl
from jax.experimental.pallas import tpu as pltpu

D_MODEL = 1024
DEPTH = 2
PAST_LEN = 4096
WIDTH_A = 512
WIDTH_B = 512
MIX_WIDTH = WIDTH_A + WIDTH_B
SSM_GROUP_CH = 16
SSM_GROUPS = WIDTH_A // SSM_GROUP_CH
SSM_STATE = 64
STATE_W = SSM_GROUPS * SSM_STATE
POOL_WINDOWS = (2, 4, 8, 16)
POOL_GROUP_CH = WIDTH_B // len(POOL_WINDOWS)
POOL_HIST = max(POOL_WINDOWS) - 1
D_FF = 4 * D_MODEL
RMS_EPS = 1e-5

TILE_ROWS = 512
FF_CHUNK = 1024
OUT_CHUNK = 256
MXU_TILE = 256
BU_TILE = MXU_TILE
BU_GROUPS = BU_TILE // SSM_STATE
BU_K = 128
N_BU_HALF = STATE_W // BU_TILE
RD_GROUPS = 8
RD_K = RD_GROUPS * SSM_STATE
RD_N = RD_GROUPS * SSM_GROUP_CH
N_RD = SSM_GROUPS // RD_GROUPS
SCAN_LANES = 128
VMEM_LIMIT_BYTES = 60 * 1024 * 1024

_BF16 = jnp.bfloat16
_F32 = jnp.float32


def _rms_scale(x):
    ms = jnp.mean(x * x, axis=-1, keepdims=True)
    return x * lax.rsqrt(ms + RMS_EPS)


def _bu_k_block(n):
    return (n % N_BU_HALF) * BU_GROUPS * SSM_GROUP_CH // BU_K


def _layer_kernel(nb, tt, n_p, per_group, n_tiles, final_norm,
                  xp_hbm, xs_hbm, h0re_ref, h0im_ref, hist0_ref,
                  gfin_ref, win_ref, wp_ref, a2re_ref, a2im_ref,
                  wrre_ref, wrim_ref, wd_ref, d_ref, wglu_ref, wpool_ref, pscale_ref,
                  wout_ref, wff1_ref, wff2_ref,
                  yp_hbm, ys_hbm, hre_out, him_out, hist_out,
                  hre, him, ext, hs, x1_buf, hf_buf, mid_buf, xbuf, ybuf, sem_in, sem_out):
    rows = nb * tt
    half = rows // 2
    hist_rows = POOL_HIST * nb
    i = pl.program_id(0)
    live = i < n_tiles

    slot = i % 2
    yslot = 1 - slot
    tile_m = jnp.minimum(i, n_tiles - 1)
    samp_m = jnp.maximum(tile_m - n_p, 0)
    is_prompt_m = tile_m < n_p
    seg = jnp.where(is_prompt_m, 0, 1 + samp_m // per_group)
    first_of_seg = jnp.where(is_prompt_m, tile_m == 0, samp_m % per_group == 0)

    def tile_copies(tile, s, buf, sem, p_hbm, s_hbm, to_vmem, op):
        def run(hbm, row0, t0):
            for b in range(nb):
                win = hbm.at[row0 + b, pl.ds(pl.multiple_of(t0, tt), tt), :]
                vm = buf.at[s, :, b, :]
                c = pltpu.make_async_copy(win, vm, sem.at[s]) if to_vmem else \
                    pltpu.make_async_copy(vm, win, sem.at[s])
                getattr(c, op)()

        @pl.when(tile < n_p)
        def _():
            run(p_hbm, 0, tile * tt)

        @pl.when(tile >= n_p)
        def _():
            j = tile - n_p
            run(s_hbm, (j // per_group) * nb, (j % per_group) * tt)

    def x_tile(tile, s, op):
        tile_copies(tile, s, xbuf, sem_in, xp_hbm, xs_hbm, True, op)

    def y_tile(tile, s, op):
        tile_copies(tile, s, ybuf, sem_out, yp_hbm, ys_hbm, False, op)

    @pl.when(i == 0)
    def _():
        x_tile(i, 0, "start")
        x1_buf[...] = jnp.zeros((rows, D_MODEL), _F32)
        hf_buf[...] = jnp.zeros((rows, D_MODEL), _BF16)

    @pl.when(first_of_seg & live)
    def _():
        hre[...] = h0re_ref[seg]
        him[...] = h0im_ref[seg]
        ext[0:hist_rows, :] = hist0_ref[seg]

    @pl.when(i + 1 < n_tiles)
    def _():
        x_tile(i + 1, yslot, "start")

    @pl.when(i < n_tiles)
    def _():
        x_tile(i, slot, "wait")

    @pl.when(i >= 3)
    def _():
        y_tile(i - 3, yslot, "wait")

    x1_prev = x1_buf[...]
    hf = hf_buf[...]

    def mlp_up(c):
        cs = slice(c * FF_CHUNK, (c + 1) * FF_CHUNK)
        mid = jnp.dot(hf, wff1_ref[:, cs], preferred_element_type=_F32)
        mid_buf[:, cs] = jnp.square(jnp.maximum(mid, 0.0)).astype(_BF16)

    def mlp_down(n):
        ns = slice(n * OUT_CHUNK, (n + 1) * OUT_CHUNK)
        return x1_prev[:, ns] + jnp.dot(mid_buf[...], wff2_ref[:, ns], preferred_element_type=_F32)

    mlp_up(0)

    x = xbuf[slot].reshape(rows, D_MODEL)
    hn = _rms_scale(x).astype(_BF16)
    proj = jnp.dot(hn, win_ref[...], preferred_element_type=_F32)
    ua = proj[:, :WIDTH_A]
    ub = proj[:, WIDTH_A:]
    ua_pairs = ua.reshape(tt // 2, 2, nb, WIDTH_A)
    ue_bf = ua_pairs[:, 0].reshape(half, WIDTH_A).astype(_BF16)
    uo_bf = ua_pairs[:, 1].reshape(half, WIDTH_A).astype(_BF16)

    mlp_up(1)

    for n in range(2 * N_BU_HALF):
        ks = slice(_bu_k_block(n) * BU_K, (_bu_k_block(n) + 1) * BU_K)
        lhs = jnp.concatenate([ue_bf[:, ks], uo_bf[:, ks]], axis=1)
        hs[nb:nb + half, n * BU_TILE:(n + 1) * BU_TILE] = jnp.dot(
            lhs, wp_ref[n], preferred_element_type=_F32)

    lc = SCAN_LANES
    n_lane_chunks = STATE_W // lc
    always = i >= 0

    def scan_lane_chunk(c, chain):
        re_sl = slice(c * lc, (c + 1) * lc)
        im_sl = slice(STATE_W + c * lc, STATE_W + (c + 1) * lc)
        a_re = jnp.broadcast_to(a2re_ref[:, re_sl], (nb, lc))
        a_im = jnp.broadcast_to(a2im_ref[:, re_sl], (nb, lc))
        h_r0 = hre[:, re_sl]
        h_i0 = him[:, re_sl]
        hs[0:nb, re_sl] = h_r0
        hs[0:nb, im_sl] = h_i0
        h_r, h_i = h_r0, h_i0
        if chain is not None:
            h_r = jnp.where(always, h_r, chain)
        for k in range(tt // 2):
            rs = slice((k + 1) * nb, (k + 2) * nb)
            n_r = a_re * h_r - a_im * h_i + hs[rs, re_sl]
            n_i = a_re * h_i + a_im * h_r + hs[rs, im_sl]
            hs[rs, re_sl] = n_r
            hs[rs, im_sl] = n_i
            h_r, h_i = n_r, n_i
        hre[:, re_sl] = jnp.where(live, h_r, h_r0)
        him[:, re_sl] = jnp.where(live, h_i, h_i0)
        return h_r

    chain = None
    for c in range(n_lane_chunks // 2):
        chain = scan_lane_chunk(c, chain)

    mlp_up(2)
    mlp_up(3)

    for c in range(n_lane_chunks // 2, n_lane_chunks):
        chain = scan_lane_chunk(c, chain)
    hre_out[seg] = hre[...]
    him_out[seg] = him[...]
    y_odd, y_even = [], []
    for m in range(N_RD):
        h_re = hs[:, m * RD_K:(m + 1) * RD_K].astype(_BF16)
        h_im = hs[:, STATE_W + m * RD_K:STATE_W + (m + 1) * RD_K].astype(_BF16)
        out = (jnp.dot(h_re, wrre_ref[m], preferred_element_type=_F32)
               + jnp.dot(h_im, wrim_ref[m], preferred_element_type=_F32))
        y_odd.append(out[nb:, :RD_N])
        y_even.append(out[:half, RD_N:])
    direct = [jnp.dot(ue_bf[:, m * MXU_TILE:(m + 1) * MXU_TILE], wd_ref[m],
                      preferred_element_type=_F32) for m in range(WIDTH_A // MXU_TILE)]
    y_even = jnp.concatenate(y_even, axis=1) + jnp.concatenate(direct, axis=1)
    y_odd = jnp.concatenate(y_odd, axis=1)
    ys = jnp.stack([y_even.reshape(tt // 2, nb, WIDTH_A), y_odd.reshape(tt // 2, nb, WIDTH_A)],
                   axis=1).reshape(rows, WIDTH_A)
    y = ys + d_ref[...] * ua
    z = jax.nn.gelu(y)
    out_a = z * jax.nn.sigmoid(jnp.dot(z.astype(_BF16), wglu_ref[...], preferred_element_type=_F32))

    y_cols = [mlp_down(0), mlp_down(1)]

    ext[hist_rows:hist_rows + rows, :] = ub
    t0 = jnp.where(is_prompt_m, tile_m * tt, PAST_LEN + (samp_m % per_group) * tt)
    pos = t0 + lax.broadcasted_iota(jnp.int32, (rows, POOL_GROUP_CH), 0) // nb
    n_seen = (pos + 1).astype(_F32)
    diffs = []
    for k, w in enumerate(POOL_WINDOWS):
        ls = slice(k * POOL_GROUP_CH, (k + 1) * POOL_GROUP_CH)
        wsum = ext[(POOL_HIST - (w - 1)) * nb:hist_rows + rows, ls]
        shift = 1
        while shift < w:
            wsum = wsum[shift * nb:] + wsum[:-shift * nb]
            shift *= 2
        cnt = jnp.minimum(n_seen, float(w))
        diffs.append((wsum / cnt - ub[:, ls]).astype(_BF16))
    yb_parts = [jnp.dot(jnp.concatenate(diffs[2 * m:2 * m + 2], axis=1), wpool_ref[m],
                        preferred_element_type=_F32) for m in range(len(POOL_WINDOWS) // 2)]
    yb = jnp.concatenate(yb_parts, axis=1) * pscale_ref[...]
    new_hist = jnp.where(live, ext[rows:rows + hist_rows, :], ext[0:hist_rows, :])
    hist_out[seg] = new_hist
    ext[0:hist_rows, :] = new_hist

    mixed = jnp.concatenate([out_a, yb], axis=1)
    x1 = x + jnp.dot(mixed.astype(_BF16), wout_ref[...], preferred_element_type=_F32)

    y_cols += [mlp_down(2), mlp_down(3)]
    if final_norm:
        ssq = sum(jnp.sum(c * c, axis=-1, keepdims=True) for c in y_cols)
        r = lax.rsqrt(ssq * (1.0 / D_MODEL) + RMS_EPS)
        acc = jnp.concatenate([c * gfin_ref[:, n * OUT_CHUNK:(n + 1) * OUT_CHUNK]
                               for n, c in enumerate(y_cols)], axis=1) * r
    else:
        acc = jnp.concatenate(y_cols, axis=1)
    ybuf[yslot] = acc.reshape(tt, nb, D_MODEL)
    x1_buf[...] = x1
    hf_buf[...] = _rms_scale(x1).astype(_BF16)

    @pl.when(i >= 1)
    def _():
        y_tile(i - 1, yslot, "start")

    @pl.when(i == n_tiles)
    def _():
        y_tile(i - 2, slot, "wait")
        y_tile(i - 1, yslot, "wait")


def _layer_call(xp, xs, h0_re, h0_im, hist0, weights, layer, *, final_norm):
    nb, seq_p, _ = xp.shape
    bs, seq_s, _ = xs.shape
    tt = TILE_ROWS // nb
    assert seq_p % tt == 0 and seq_s % tt == 0 and bs % nb == 0 and tt >= POOL_HIST and tt % 8 == 0
    assert D_FF // FF_CHUNK == 4 and D_MODEL // OUT_CHUNK == 4, "the kernel body is written for 4 + 4 MLP chunks"
    n_p = seq_p // tt
    per_group = seq_s // tt
    n_seg = 1 + bs // nb
    n_tiles = n_p + (bs // nb) * per_group
    assert n_tiles >= 3
    hist_rows = POOL_HIST * nb

    def resident(w):
        zeros = (0,) * (w.ndim - 1)
        return pl.BlockSpec((None,) + w.shape[1:], lambda i: (layer,) + zeros,
                            pipeline_mode=pl.Buffered(1))

    whole = pl.BlockSpec(memory_space=pltpu.VMEM)
    hbm = pl.BlockSpec(memory_space=pl.ANY)
    out_shape = (
        jax.ShapeDtypeStruct(xp.shape, _F32),
        jax.ShapeDtypeStruct(xs.shape, _F32),
        jax.ShapeDtypeStruct((n_seg, nb, STATE_W), _F32),
        jax.ShapeDtypeStruct((n_seg, nb, STATE_W), _F32),
        jax.ShapeDtypeStruct((n_seg, hist_rows, WIDTH_B), _F32),
    )
    return pl.pallas_call(
        functools.partial(_layer_kernel, nb, tt, n_p, per_group, n_tiles, final_norm),
        grid=(n_tiles + 1,),
        in_specs=[hbm, hbm, whole, whole, whole] + [resident(w) for w in weights],
        out_specs=(hbm, hbm, whole, whole, whole),
        out_shape=out_shape,
        scratch_shapes=[
            pltpu.VMEM((nb, STATE_W), _F32),
            pltpu.VMEM((nb, STATE_W), _F32),
            pltpu.VMEM((hist_rows + TILE_ROWS, WIDTH_B), _F32),
            pltpu.VMEM((TILE_ROWS // 2 + nb, 2 * STATE_W), _F32),
            pltpu.VMEM((TILE_ROWS, D_MODEL), _F32),
            pltpu.VMEM((TILE_ROWS, D_MODEL), _BF16),
            pltpu.VMEM((TILE_ROWS, D_FF), _BF16),
            pltpu.VMEM((2, tt, nb, D_MODEL), _F32),
            pltpu.VMEM((2, tt, nb, D_MODEL), _F32),
            pltpu.SemaphoreType.DMA((2,)),
            pltpu.SemaphoreType.DMA((2,)),
        ],
        compiler_params=pltpu.CompilerParams(
            dimension_semantics=("arbitrary",), vmem_limit_bytes=VMEM_LIMIT_BYTES),
    )(xp, xs, h0_re, h0_im, hist0, *weights)


def _prep_weights(g_mix, g_ffn, g_final, w_in, lam_re, lam_im, log_dt, b_re, b_im, c_re, c_im,
                  d, w_glu, w_pool, pool_scale, w_out, w_ff1, w_ff2):
    depth = w_in.shape[0]
    dt = jnp.exp(log_dt)[..., None]
    mag = jnp.exp(lam_re * dt)
    a_re = mag * jnp.cos(lam_im * dt)
    a_im = mag * jnp.sin(lam_im * dt)
    den = lam_re * lam_re + lam_im * lam_im
    q_re = ((a_re - 1.0) * lam_re + a_im * lam_im) / den
    q_im = (a_im * lam_re - (a_re - 1.0) * lam_im) / den
    bb_re = q_re[..., None] * b_re - q_im[..., None] * b_im
    bb_im = q_re[..., None] * b_im + q_im[..., None] * b_re
    ab_re = a_re[..., None] * bb_re - a_im[..., None] * bb_im
    ab_im = a_re[..., None] * bb_im + a_im[..., None] * bb_re
    a2_re = a_re * a_re - a_im * a_im
    a2_im = 2.0 * a_re * a_im
    ca_re = c_re * a_re[:, :, None, :] - c_im * a_im[:, :, None, :]
    ca_im = c_re * a_im[:, :, None, :] + c_im * a_re[:, :, None, :]
    direct = (jnp.einsum('lgcp,lgpd->lgcd', c_re, bb_re)
              - jnp.einsum('lgcp,lgpd->lgcd', c_im, bb_im))

    def bu_tiles(bb):
        t = jnp.swapaxes(bb, -1, -2).reshape(depth, N_BU_HALF, BU_GROUPS, SSM_GROUP_CH, SSM_STATE)
        t = jnp.einsum('lnqcp,qr->lnqcrp', t, jnp.eye(BU_GROUPS, dtype=_F32))
        t = t.reshape(depth, N_BU_HALF, BU_GROUPS * SSM_GROUP_CH, BU_TILE)
        z = jnp.zeros_like(t)
        first = (jnp.arange(N_BU_HALF) % 2 == 0)[None, :, None, None]
        return jnp.where(first, jnp.concatenate([t, z], axis=2), jnp.concatenate([z, t], axis=2))

    def pair_tiles(ab, bb):
        return jnp.concatenate([bu_tiles(ab), bu_tiles(bb)], axis=2)

    def rd_block(cp):
        t = cp.reshape(depth, N_RD, RD_GROUPS, SSM_GROUP_CH, SSM_STATE)
        t = jnp.einsum('lmgcp,gh->lmgphc', t, jnp.eye(RD_GROUPS, dtype=_F32))
        return t.reshape(depth, N_RD, RD_K, RD_N)

    def rd_tiles(cp, cap):
        return jnp.concatenate([rd_block(cp), rd_block(cap)], axis=-1)

    n_dt = WIDTH_A // MXU_TILE
    g_dt = MXU_TILE // SSM_GROUP_CH
    wd = jnp.einsum('lmgcd,gh->lmgdhc', direct.reshape(depth, n_dt, g_dt, SSM_GROUP_CH, SSM_GROUP_CH),
                    jnp.eye(g_dt, dtype=_F32)).reshape(depth, n_dt, MXU_TILE, MXU_TILE)
    n_pp = len(POOL_WINDOWS) // 2
    wpool2 = jnp.einsum('lmkcd,kj->lmkcjd', w_pool.reshape(depth, n_pp, 2, POOL_GROUP_CH, POOL_GROUP_CH),
                        jnp.eye(2, dtype=_F32)).reshape(depth, n_pp, 2 * POOL_GROUP_CH, 2 * POOL_GROUP_CH)
    wp = jnp.concatenate([pair_tiles(ab_re, bb_re), pair_tiles(ab_im, bb_im)], axis=1)
    row = lambda v: v.reshape(depth, 1, -1).astype(_F32)
    g_fin = jnp.broadcast_to(g_final.reshape(1, 1, -1), (depth, 1, D_MODEL)).astype(_F32)
    w_in_g = g_mix.astype(_F32)[:, :, None] * w_in
    w_ff1_g = g_ffn.astype(_F32)[:, :, None] * w_ff1
    return (g_fin, w_in_g.astype(_BF16), wp.astype(_BF16),
            row(a2_re), row(a2_im), rd_tiles(c_re, ca_re).astype(_BF16),
            rd_tiles(-c_im, -ca_im).astype(_BF16), wd.astype(_BF16),
            row(d), w_glu.astype(_BF16), wpool2.astype(_BF16), row(pool_scale),
            w_out.astype(_BF16), w_ff1_g.astype(_BF16), w_ff2.astype(_BF16))


def _hist_to_segments(hist, nb):
    b, h, c = hist.shape
    return jnp.swapaxes(hist.reshape(b // nb, nb, h, c), 1, 2).reshape(b // nb, h * nb, c)


def _hist_from_segments(hist_seg, nb):
    g, _, c = hist_seg.shape
    return jnp.swapaxes(hist_seg.reshape(g, POOL_HIST, nb, c), 1, 2).reshape(g * nb, POOL_HIST, c)


def kernel(x_prompt, x_sample, state_ssm_re, state_ssm_im, state_pool, g_mix, g_ffn, w_in,
           ssm_lambda_re, ssm_lambda_im, ssm_log_dt, ssm_b_re, ssm_b_im, ssm_c_re, ssm_c_im,
           ssm_d, w_glu, w_pool, pool_scale, w_out, w_ff1, w_ff2, g_final):
    nb = x_prompt.shape[0]
    bs = x_sample.shape[0]
    weights = _prep_weights(g_mix, g_ffn, g_final, w_in, ssm_lambda_re, ssm_lambda_im, ssm_log_dt,
                            ssm_b_re, ssm_b_im, ssm_c_re, ssm_c_im, ssm_d, w_glu, w_pool,
                            pool_scale, w_out, w_ff1, w_ff2)
    xp, xs = x_prompt, x_sample
    zero_state = jnp.zeros((1, nb, STATE_W), _F32)
    zero_hist = jnp.zeros((1, POOL_HIST * nb, WIDTH_B), _F32)
    states = lambda v: v.reshape(bs // nb, nb, STATE_W).astype(_F32)
    outs = {k: [] for k in ("p_re", "p_im", "p_pool", "s_re", "s_im", "s_pool")}
    for l in range(DEPTH):
        xp, xs, h_re, h_im, hist = _layer_call(
            xp, xs,
            jnp.concatenate([zero_state, states(state_ssm_re[l])]),
            jnp.concatenate([zero_state, states(state_ssm_im[l])]),
            jnp.concatenate([zero_hist, _hist_to_segments(state_pool[l].astype(_F32), nb)]),
            weights, l, final_norm=(l == DEPTH - 1))
        outs["p_re"].append(h_re[0].reshape(nb, SSM_GROUPS, SSM_STATE))
        outs["p_im"].append(h_im[0].reshape(nb, SSM_GROUPS, SSM_STATE))
        outs["p_pool"].append(_hist_from_segments(hist[:1], nb))
        outs["s_re"].append(h_re[1:].reshape(bs, SSM_GROUPS, SSM_STATE))
        outs["s_im"].append(h_im[1:].reshape(bs, SSM_GROUPS, SSM_STATE))
        outs["s_pool"].append(_hist_from_segments(hist[1:], nb))
    return (xp, xs,
            jnp.stack(outs["p_re"]), jnp.stack(outs["p_im"]), jnp.stack(outs["p_pool"]),
            jnp.stack(outs["s_re"]), jnp.stack(outs["s_im"]), jnp.stack(outs["s_pool"]))
```
